```python
import jax
import jax.numpy as jnp
from jax import lax
import numpy as np

D_MODEL = 4096
BATCH = 2
SEQ = 4096
DEPTH = 2

GRID_W = 64
CTX_LEN = 256
N_MIXERS = 2
N_EVEN = (DEPTH + 1) // 2
N_ODD = DEPTH // 2

ML_HEADS = 8
ML_QK = D_MODEL // 2
ML_V = D_MODEL
ML_DK = ML_QK // ML_HEADS
ML_DV = ML_V // ML_HEADS
ML_CHUNK = 64
ML_PROJ = 2 * ML_QK + 2 * ML_V + 4 * ML_HEADS

ATT_HEADS = 32
ATT_KV = 8
HEAD_DIM = D_MODEL // ATT_HEADS
ATT_GROUP = ATT_HEADS // ATT_KV
ATT_Q_COLS = ATT_HEADS * HEAD_DIM
ATT_KV_COLS = ATT_KV * HEAD_DIM
ATT_PROJ = ATT_Q_COLS + 2 * ATT_KV_COLS
WINDOW = 128
Q_BLOCK = 128
BAND_SIDE = -(-WINDOW // Q_BLOCK)
ROPE_THETA = 10000.0

D_FF = 11008
N_EXPERTS = 8
TOP_K = 2
D_FF_EXPERT = 4096

ALPHA = (2 * DEPTH) ** 0.25
BETA = (8 * DEPTH) ** -0.25
LN_EPS = 1e-5
RMS_EPS = 1e-6

kernel_name = 'hybrid_mlstm_swa_moe_dit'


def _layer_norm(x, g, b):
    xf = x.astype(jnp.float32)
    mu = jnp.mean(xf, axis=-1, keepdims=True)
    var = jnp.mean(jnp.square(xf - mu), axis=-1, keepdims=True)
    y = (xf - mu) * lax.rsqrt(var + LN_EPS)
    return (y * g.astype(jnp.float32) + b.astype(jnp.float32)).astype(x.dtype)


def _modulate(x, shift, scale):
    return x * (1 + scale) + shift


def _adaln(cond, w, b):
    return jnp.split(jax.nn.silu(cond) @ w + b, 6, axis=-1)


def _mlstm_inputs(p, gate_b):
    B, T, _ = p.shape
    q, k, v, o, g = jnp.split(p, [ML_QK, 2 * ML_QK, 2 * ML_QK + ML_V, 2 * ML_QK + 2 * ML_V], axis=-1)

    def heads(a, d):
        return a.reshape(B, T, ML_HEADS, d).transpose(0, 2, 1, 3).astype(jnp.float32)

    q = heads(q, ML_DK) * (ML_DK ** -0.5)
    k = heads(k, ML_DK)
    v = heads(v, ML_DV)
    g = (g + gate_b).astype(jnp.float32).reshape(B, T, 4, ML_HEADS).transpose(2, 0, 3, 1)
    fwd = (g[0], jax.nn.log_sigmoid(g[1]))
    bwd = (g[2], jax.nn.log_sigmoid(g[3]))
    return q, k, v, o, fwd, bwd


def _mlstm_scan(q, k, v, ig, lf, state):
    B, H, T, _ = q.shape
    nc = T // ML_CHUNK

    def to_chunks(a):
        return jnp.moveaxis(a.reshape((B, H, nc, ML_CHUNK) + a.shape[3:]), 2, 0)

    xs = (to_chunks(q), to_chunks(k), to_chunks(v), to_chunks(ig), to_chunks(lf))
    past = jnp.tril(jnp.ones((ML_CHUNK, ML_CHUNK), dtype=bool))

    def step(carry, chunk):
        C, nv, m = carry
        qc, kc, vc, igc, lfc = chunk
        b = jnp.cumsum(lfc, axis=-1)
        d = b[..., :, None] - b[..., None, :] + igc[..., None, :]
        d = jnp.where(past, d, -jnp.inf)
        inter = b + m[..., None]
        m_t = jnp.maximum(inter, jnp.max(d, axis=-1))
        s = jnp.einsum('bhtd,bhsd->bhts', qc, kc) * jnp.exp(d - m_t[..., None])
        g = jnp.exp(inter - m_t)
        num = g[..., None] * jnp.einsum('bhtd,bhde->bhte', qc, C) + jnp.einsum('bhts,bhse->bhte', s, vc)
        den = g * jnp.einsum('bhtd,bhd->bht', qc, nv) + jnp.sum(s, axis=-1)
        h = num / jnp.maximum(jnp.abs(den), jnp.exp(-m_t))[..., None]
        b_last = b[..., -1]
        w = b_last[..., None] - b + igc
        m_new = jnp.maximum(b_last + m, jnp.max(w, axis=-1))
        decay = jnp.exp(b_last + m - m_new)
        w = jnp.exp(w - m_new[..., None])
        C = decay[..., None, None] * C + jnp.einsum('bhs,bhsd,bhse->bhde', w, kc, vc)
        nv = decay[..., None] * nv + jnp.einsum('bhs,bhsd->bhd', w, kc)
        return (C, nv, m_new), h

    state_out, h = lax.scan(step, state, xs)
    h = jnp.moveaxis(h, 0, 2).reshape(B, H, T, ML_DV)
    return h, state_out


def _mlstm_out(h, o, norm_g, w_out, dtype):
    B, H, T, DV = h.shape
    h = h * lax.rsqrt(jnp.mean(jnp.square(h), axis=-1, keepdims=True) + RMS_EPS)
    h = h.transpose(0, 2, 1, 3).reshape(B, T, H * DV) * norm_g.astype(jnp.float32)
    h = h * jax.nn.sigmoid(o.astype(jnp.float32))
    return h.astype(dtype) @ w_out


def _mlstm_mixer(a_ctx, a_lat, w_in, gate_b, norm_g, w_out, need_ctx):
    qc, kc, vc, oc, gfc, gbc = _mlstm_inputs(a_ctx @ w_in, gate_b)
    ql, kl, vl, ol, gfl, gbl = _mlstm_inputs(a_lat @ w_in, gate_b)
    B = a_lat.shape[0]
    init = (jnp.zeros((B, ML_HEADS, ML_DK, ML_DV), jnp.float32),
            jnp.zeros((B, ML_HEADS, ML_DK), jnp.float32),
            jnp.zeros((B, ML_HEADS), jnp.float32))
    hc_f, st_f = _mlstm_scan(qc, kc, vc, gfc[0], gfc[1], init)
    hl_f, _ = _mlstm_scan(ql, kl, vl, gfl[0], gfl[1], st_f)
    fl = lambda a: jnp.flip(a, axis=2)
    hc_b, st_b = _mlstm_scan(fl(qc), fl(kc), fl(vc), fl(gbc[0]), fl(gbc[1]), init)
    hl_b, _ = _mlstm_scan(fl(ql), fl(kl), fl(vl), fl(gbl[0]), fl(gbl[1]), st_b)
    y_lat = _mlstm_out(hl_f + fl(hl_b), ol, norm_g, w_out, a_lat.dtype)
    y_ctx = _mlstm_out(hc_f + fl(hc_b), oc, norm_g, w_out, a_ctx.dtype) if need_ctx else None
    return y_ctx, y_lat


def _rope_1d(x, pos):
    half = x.shape[-1] // 2
    freqs = ROPE_THETA ** (-jnp.arange(half, dtype=jnp.float32) / half)
    ang = pos.astype(jnp.float32)[:, None] * freqs[None, :]
    cos = jnp.cos(ang)[:, None, :].astype(x.dtype)
    sin = jnp.sin(ang)[:, None, :].astype(x.dtype)
    x1, x2 = x[..., :half], x[..., half:]
    return jnp.concatenate([x1 * cos - x2 * sin, x2 * cos + x1 * sin], axis=-1)


def _axial_rope(x, pos_row, pos_col):
    h = x.shape[-1] // 2
    return jnp.concatenate([_rope_1d(x[..., :h], pos_row), _rope_1d(x[..., h:], pos_col)], axis=-1)


def _sink_logits(sink, shape):
    s = sink.astype(jnp.float32).reshape(ATT_KV, ATT_GROUP)
    s = s.reshape((1,) * (len(shape) - 4) + (ATT_KV, ATT_GROUP, 1, 1))
    return jnp.broadcast_to(s, shape[:-1] + (1,))


def _window_attention(q, k, v, k_c, v_c, sink):
    B, T = q.shape[0], q.shape[1]
    C = k_c.shape[1]
    nb = T // Q_BLOCK
    nband = 2 * BAND_SIDE + 1
    nloc = nband * Q_BLOCK
    qb = q.reshape(B, nb, Q_BLOCK, ATT_KV, ATT_GROUP, HEAD_DIM)

    def band(a):
        ab = a.reshape(B, nb, Q_BLOCK, ATT_KV, HEAD_DIM)
        ab = jnp.pad(ab, ((0, 0), (BAND_SIDE, BAND_SIDE), (0, 0), (0, 0), (0, 0)))
        return jnp.concatenate([ab[:, o:o + nb] for o in range(nband)], axis=2)

    kband, vband = band(k), band(v)
    qpos = jnp.arange(T).reshape(nb, Q_BLOCK)
    kpos = (jnp.arange(nb)[:, None] - BAND_SIDE) * Q_BLOCK + jnp.arange(nloc)[None, :]
    valid = ((jnp.abs(qpos[:, :, None] - kpos[:, None, :]) <= WINDOW)
             & (kpos[:, None, :] >= 0) & (kpos[:, None, :] < T))
    scale = HEAD_DIM ** -0.5
    s_loc = jnp.einsum('bnqkgd,bnskd->bnkgqs', qb, kband).astype(jnp.float32) * scale
    s_loc = jnp.where(valid[None, :, None, None], s_loc, -jnp.inf)
    s_ctx = jnp.einsum('bnqkgd,bckd->bnkgqc', qb, k_c).astype(jnp.float32) * scale
    s_sink = _sink_logits(sink, s_loc.shape)
    p = jax.nn.softmax(jnp.concatenate([s_loc, s_ctx, s_sink], axis=-1), axis=-1).astype(v.dtype)
    o = (jnp.einsum('bnkgqs,bnskd->bnqkgd', p[..., :nloc], vband)
         + jnp.einsum('bnkgqc,bckd->bnqkgd', p[..., nloc:nloc + C], v_c))
    return o.reshape(B, T, ATT_HEADS * HEAD_DIM)


def _context_attention(q_c, k_c, v_c, sink):
    B, C = q_c.shape[0], q_c.shape[1]
    qg = q_c.reshape(B, C, ATT_KV, ATT_GROUP, HEAD_DIM)
    s = jnp.einsum('bqkgd,bskd->bkgqs', qg, k_c).astype(jnp.float32) * (HEAD_DIM ** -0.5)
    p = jax.nn.softmax(jnp.concatenate([s, _sink_logits(sink, s.shape)], axis=-1), axis=-1).astype(v_c.dtype)
    o = jnp.einsum('bkgqs,bskd->bqkgd', p[..., :C], v_c)
    return o.reshape(B, C, ATT_HEADS * HEAD_DIM)


def _attention_mixer(a_ctx, a_lat, w_in, sink, w_out, pos_row, pos_col, need_ctx):
    B, T, _ = a_lat.shape
    C = a_ctx.shape[1]
    p = a_lat @ w_in
    q = _axial_rope(p[..., :ATT_Q_COLS].reshape(B, T, ATT_HEADS, HEAD_DIM), pos_row, pos_col)
    k = _axial_rope(p[..., ATT_Q_COLS:ATT_Q_COLS + ATT_KV_COLS].reshape(B, T, ATT_KV, HEAD_DIM), pos_row, pos_col)
    v = p[..., ATT_Q_COLS + ATT_KV_COLS:].reshape(B, T, ATT_KV, HEAD_DIM)
    kv_c = a_ctx @ w_in[:, ATT_Q_COLS:]
    k_c = kv_c[..., :ATT_KV_COLS].reshape(B, C, ATT_KV, HEAD_DIM)
    v_c = kv_c[..., ATT_KV_COLS:].reshape(B, C, ATT_KV, HEAD_DIM)
    y_lat = _window_attention(q, k, v, k_c, v_c, sink) @ w_out
    y_ctx = None
    if need_ctx:
        q_c = (a_ctx @ w_in[:, :ATT_Q_COLS]).reshape(B, C, ATT_HEADS, HEAD_DIM)
        y_ctx = _context_attention(q_c, k_c, v_c, sink) @ w_out
    return y_ctx, y_lat


def _swiglu(h, w_gu, w_down):
    g, u = jnp.split(h @ w_gu, 2, axis=-1)
    return (jax.nn.silu(g) * u) @ w_down


def _moe_swiglu(h, w_router, w_gu, w_down):
    B, T, D = h.shape
    hf = h.reshape(B * T, D)
    logits = (hf @ w_router).astype(jnp.float32)
    top_v, top_i = lax.top_k(logits, TOP_K)
    top_w = jax.nn.softmax(top_v, axis=-1)
    gates = jnp.sum(jax.nn.one_hot(top_i, N_EXPERTS, dtype=jnp.float32) * top_w[..., None], axis=1).astype(h.dtype)
    y = jnp.zeros_like(hf)
    for e in range(N_EXPERTS):
        y = y + gates[:, e:e + 1] * _swiglu(hf, w_gu[e], w_down[e])
    return y.reshape(B, T, D)


def setup_inputs(seed: int = 0) -> dict:
    key = jax.random.key(seed)
    ks = jax.random.split(key, 20)

    def nrm(k, shape, scale):
        return jax.random.normal(k, shape, jnp.float32) * scale

    gate_base = jnp.concatenate([jnp.zeros((ML_HEADS,)), jnp.linspace(3.0, 6.0, ML_HEADS),
                                 jnp.zeros((ML_HEADS,)), jnp.linspace(3.0, 6.0, ML_HEADS)]).astype(jnp.float32)
    return {
        'x': nrm(ks[0], (BATCH, SEQ, D_MODEL), 1.0),
        'c': nrm(ks[1], (BATCH, D_MODEL), 1.0),
        'ctx': nrm(ks[2], (BATCH, CTX_LEN, D_MODEL), 1.0),
        'c_ctx': nrm(ks[3], (D_MODEL,), 1.0),
        'ada_w': nrm(ks[4], (DEPTH, D_MODEL, 6 * D_MODEL), 0.5 * D_MODEL ** -0.5),
        'ada_b': nrm(ks[5], (DEPTH, 6 * D_MODEL), 0.02),
        'ln_g': 1.0 + nrm(ks[6], (DEPTH, 2, D_MODEL), 0.02),
        'ln_b': nrm(ks[7], (DEPTH, 2, D_MODEL), 0.02),
        'ml_w_in': nrm(ks[8], (N_EVEN, D_MODEL, ML_PROJ), D_MODEL ** -0.5),
        'ml_gate_b': gate_base[None, :] + nrm(ks[9], (N_EVEN, 4 * ML_HEADS), 0.1),
        'ml_norm_g': 1.0 + nrm(ks[10], (N_EVEN, ML_V), 0.02),
        'ml_w_out': nrm(ks[11], (N_EVEN, ML_V, D_MODEL), BETA * ML_V ** -0.5),
        'att_w_in': nrm(ks[12], (N_ODD, D_MODEL, ATT_PROJ), D_MODEL ** -0.5),
        'att_sink': nrm(ks[13], (N_ODD, ATT_HEADS), 0.5),
        'att_w_out': nrm(ks[14], (N_ODD, ATT_Q_COLS, D_MODEL), BETA * ATT_Q_COLS ** -0.5),
        'ffn_w_gu': nrm(ks[15], (N_EVEN, D_MODEL, 2 * D_FF), D_MODEL ** -0.5),
        'ffn_w_down': nrm(ks[16], (N_EVEN, D_FF, D_MODEL), BETA * D_FF ** -0.5),
        'moe_router': nrm(ks[17], (N_ODD, D_MODEL, N_EXPERTS), D_MODEL ** -0.5),
        'moe_w_gu': nrm(ks[18], (N_ODD, N_EXPERTS, D_MODEL, 2 * D_FF_EXPERT), D_MODEL ** -0.5),
        'moe_w_down': nrm(ks[19], (N_ODD, N_EXPERTS, D_FF_EXPERT, D_MODEL), BETA * D_FF_EXPERT ** -0.5),
    }


def reference(x, c, ctx, c_ctx, ada_w, ada_b, ln_g, ln_b, ml_w_in, ml_gate_b, ml_norm_g, ml_w_out,
              att_w_in, att_sink, att_w_out, ffn_w_gu, ffn_w_down, moe_router, moe_w_gu, moe_w_down):
    n = x.shape[1]
    rows = n // GRID_W
    pos_row = jnp.repeat(jnp.arange(rows), GRID_W)
    pos_col = jnp.tile(jnp.arange(GRID_W), rows)
    h_lat, h_ctx = x, ctx
    for i in range(DEPTH):
        last = i == DEPTH - 1
        j = i // 2
        m_lat = [m[:, None, :] for m in _adaln(c, ada_w[i], ada_b[i])]
        m_ctx = _adaln(c_ctx, ada_w[i], ada_b[i])
        a_lat = _modulate(h_lat, m_lat[0], m_lat[1])
        a_ctx = _modulate(h_ctx, m_ctx[0], m_ctx[1])
        if i % N_MIXERS == 0:
            y_ctx, y_lat = _mlstm_mixer(a_ctx, a_lat, ml_w_in[j], ml_gate_b[j], ml_norm_g[j], ml_w_out[j], not last)
        else:
            y_ctx, y_lat = _attention_mixer(a_ctx, a_lat, att_w_in[j], att_sink[j], att_w_out[j],
                                            pos_row, pos_col, not last)
        h_lat = _layer_norm(ALPHA * h_lat + m_lat[2] * y_lat, ln_g[i, 0], ln_b[i, 0])
        f_lat = _modulate(h_lat, m_lat[3], m_lat[4])
        if i % 2 == 0:
            y_lat = _swiglu(f_lat, ffn_w_gu[j], ffn_w_down[j])
        else:
            y_lat = _moe_swiglu(f_lat, moe_router[j], moe_w_gu[j], moe_w_down[j])
        h_lat = _layer_norm(ALPHA * h_lat + m_lat[5] * y_lat, ln_g[i, 1], ln_b[i, 1])
        if not last:
            h_ctx = _layer_norm(ALPHA * h_ctx + m_ctx[2] * y_ctx, ln_g[i, 0], ln_b[i, 0])
            f_ctx = _modulate(h_ctx, m_ctx[3], m_ctx[4])
            if i % 2 == 0:
                y_ctx = _swiglu(f_ctx, ffn_w_gu[j], ffn_w_down[j])
            else:
                y_ctx = _moe_swiglu(f_ctx, moe_router[j], moe_w_gu[j], moe_w_down[j])
            h_ctx = _layer_norm(ALPHA * h_ctx + m_ctx[5] * y_ctx, ln_g[i, 1], ln_b[i, 1])
    return h_lat
```

```python
import functools
import math

import jax
import jax.numpy as jnp
from jax import lax
from jax.experimental import pallas as pl
from jax.experimental.pallas import tpu as pltpu

F32 = jnp.float32
BF16 = jnp.bfloat16

GRID_W = 64
ML_HEADS = 8
ATT_HEADS = 32
ATT_KV = 8
WINDOW = 128
Q_BLOCK = 128
ROPE_THETA = 10000.0
TOP_K = 2
LN_EPS = 1e-5
RMS_EPS = 1e-6
DEPTH = 2
ALPHA = (2 * DEPTH) ** 0.25

V7X_VMEM_BYTES = 64 * 1024 * 1024
VMEM_CAP = V7X_VMEM_BYTES - 8 * 1024 * 1024
LANES = 128
MASKED = -1e30

ROW_TILE = 256
MM_ROWS = 1088
ML_CHUNK = 256
MOE_ROWS = 512


def _divisor(n, target, mult):
    best = None
    for d in range(mult, min(n, target) + 1, mult):
        if n % d == 0:
            best = d
    return best if best is not None else n


def _params(sem, vmem_bytes):
    limit = int(min(VMEM_CAP, max(32 * 1024 * 1024, vmem_bytes)))
    return pltpu.CompilerParams(dimension_semantics=sem, vmem_limit_bytes=limit)


def _nbytes(shape, dtype):
    return math.prod(shape) * jnp.dtype(dtype).itemsize


def _mm_kernel(a_ref, w_ref, o_ref):
    o_ref[...] = jnp.dot(a_ref[...], w_ref[...].astype(BF16), preferred_element_type=F32).astype(o_ref.dtype)


def _matmul(a, w, *, n_out=None, out_dtype=F32, tn=512, name="matmul"):
    M, K = a.shape
    n_out = w.shape[1] if n_out is None else n_out
    tm = _divisor(M, MM_ROWS, 16)
    tn = _divisor(n_out, tn, LANES)
    vmem = (2 * _nbytes((tm, K), BF16) + 2 * _nbytes((K, tn), F32) + _nbytes((K, tn), BF16)
            + 2 * _nbytes((tm, tn), out_dtype) + _nbytes((tm, tn), F32) + (4 << 20))
    return pl.pallas_call(
        _mm_kernel,
        out_shape=jax.ShapeDtypeStruct((M, n_out), out_dtype),
        grid=(M // tm, n_out // tn),
        in_specs=[pl.BlockSpec((tm, K), lambda i, j: (i, 0)),
                  pl.BlockSpec((K, tn), lambda i, j: (0, j))],
        out_specs=pl.BlockSpec((tm, tn), lambda i, j: (i, j)),
        compiler_params=_params(("parallel", "arbitrary"), vmem),
        name=name,
    )(a, w)


def _mm_ksplit_kernel(a_ref, w_ref, o_ref):
    k = pl.program_id(2)
    part = jnp.dot(a_ref[...], w_ref[...].astype(BF16), preferred_element_type=F32)

    @pl.when(k == 0)
    def _():
        o_ref[...] = part

    @pl.when(k != 0)
    def _():
        o_ref[...] += part


def _matmul_ksplit(a, w, *, tk, tn=256, name="matmul_ksplit"):
    M, K = a.shape
    N = w.shape[1]
    tm = _divisor(M, MM_ROWS, 16)
    tn = _divisor(N, tn, LANES)
    vmem = (2 * _nbytes((tm, tk), BF16) + 2 * _nbytes((tk, tn), F32) + _nbytes((tk, tn), BF16)
            + 3 * _nbytes((tm, tn), F32) + (4 << 20))
    return pl.pallas_call(
        _mm_ksplit_kernel,
        out_shape=jax.ShapeDtypeStruct((M, N), F32),
        grid=(M // tm, N // tn, K // tk),
        in_specs=[pl.BlockSpec((tm, tk), lambda i, j, k: (i, k)),
                  pl.BlockSpec((tk, tn), lambda i, j, k: (k, j))],
        out_specs=pl.BlockSpec((tm, tn), lambda i, j, k: (i, j)),
        compiler_params=_params(("parallel", "arbitrary", "arbitrary"), vmem),
        name=name,
    )(a, w)


def _swiglu_up_kernel(a_ref, wg_ref, wu_ref, o_ref):
    a = a_ref[...]
    g = jnp.dot(a, wg_ref[...].astype(BF16), preferred_element_type=F32)
    u = jnp.dot(a, wu_ref[...].astype(BF16), preferred_element_type=F32)
    o_ref[...] = (g * jax.nn.sigmoid(g) * u).astype(o_ref.dtype)


def _swiglu_up(a, w_gu, *, tn=256, name="swiglu_up"):
    M, K = a.shape
    F = w_gu.shape[1] // 2
    tm = _divisor(M, MM_ROWS, 16)
    tn = _divisor(F, tn, LANES)
    nf = F // tn
    vmem = (2 * _nbytes((tm, K), BF16) + 4 * _nbytes((K, tn), F32) + 2 * _nbytes((K, tn), BF16)
            + 2 * _nbytes((tm, tn), BF16) + 3 * _nbytes((tm, tn), F32) + (4 << 20))
    return pl.pallas_call(
        _swiglu_up_kernel,
        out_shape=jax.ShapeDtypeStruct((M, F), BF16),
        grid=(M // tm, nf),
        in_specs=[pl.BlockSpec((tm, K), lambda i, j: (i, 0)),
                  pl.BlockSpec((K, tn), lambda i, j: (0, j)),
                  pl.BlockSpec((K, tn), lambda i, j: (0, j + nf))],
        out_specs=pl.BlockSpec((tm, tn), lambda i, j: (i, j)),
        compiler_params=_params(("parallel", "arbitrary"), vmem),
        name=name,
    )(a, w_gu, w_gu)


def _adaln_kernel(c_ref, w_ref, b_ref, o_ref):
    c = c_ref[...]
    a = (c * jax.nn.sigmoid(c)).astype(BF16)
    o_ref[...] = jnp.dot(a, w_ref[...].astype(BF16), preferred_element_type=F32) + b_ref[...]


def _adaln(cond, w, b, *, tn=512):
    R, D = cond.shape
    N = w.shape[1]
    tn = _divisor(N, tn, LANES)
    vmem = 2 * _nbytes((D, tn), F32) + _nbytes((D, tn), BF16) + (4 << 20)
    out = pl.pallas_call(
        _adaln_kernel,
        out_shape=jax.ShapeDtypeStruct((R, N), F32),
        grid=(N // tn,),
        in_specs=[pl.BlockSpec((R, D), lambda j: (0, 0)),
                  pl.BlockSpec((D, tn), lambda j: (0, j)),
                  pl.BlockSpec((1, tn), lambda j: (0, j))],
        out_specs=pl.BlockSpec((R, tn), lambda j: (0, j)),
        compiler_params=_params(("arbitrary",), vmem),
        name="adaln",
    )(cond, w, b.reshape(1, N))
    return out.reshape(R, 6, D)


def _cond_index(n_ctx_tiles):
    def index(b, r, n_batch):
        return jnp.where(r < n_ctx_tiles, n_batch, b)
    return index


def _modulate_kernel(h_ref, m_ref, o_ref, *, shift, scale):
    m = m_ref[0]
    o_ref[0] = (h_ref[0] * (1.0 + m[scale:scale + 1]) + m[shift:shift + 1]).astype(o_ref.dtype)


def _modulate(h, mods, *, n_ctx_rows, shift, scale):
    B, S, D = h.shape
    tr = _divisor(math.gcd(S, n_ctx_rows), ROW_TILE, 16)
    nct = n_ctx_rows // tr
    return pl.pallas_call(
        functools.partial(_modulate_kernel, shift=shift, scale=scale),
        out_shape=jax.ShapeDtypeStruct((B, S, D), BF16),
        grid=(B, S // tr),
        in_specs=[pl.BlockSpec((1, tr, D), lambda b, r: (b, r, 0)),
                  pl.BlockSpec((1, 6, D), lambda b, r: (jnp.where(r < nct, B, b), 0, 0))],
        out_specs=pl.BlockSpec((1, tr, D), lambda b, r: (b, r, 0)),
        compiler_params=_params(("parallel", "arbitrary"), 0),
        name="modulate",
    )(h, mods)


def _layer_norm_rows(z, g, b):
    mu = jnp.mean(z, axis=-1, keepdims=True)
    zc = z - mu
    var = jnp.mean(zc * zc, axis=-1, keepdims=True)
    return zc * lax.rsqrt(var + LN_EPS) * g + b


def _resid_ln_kernel(h_ref, y_ref, m_ref, mn_ref, g_ref, b_ref, hn_ref, f_ref, *, gate, shift, scale):
    m = m_ref[0]
    hn = _layer_norm_rows(ALPHA * h_ref[0] + m[gate:gate + 1] * y_ref[0], g_ref[...], b_ref[...])
    hn_ref[0] = hn
    mn = mn_ref[0]
    f_ref[0] = (hn * (1.0 + mn[scale:scale + 1]) + mn[shift:shift + 1]).astype(f_ref.dtype)


def _resid_ln(h, y, mods, mods_next, ln_g, ln_b, *, n_ctx_rows, h_row_offset, gate, shift, scale):
    B, R, D = y.shape
    tr = _divisor(math.gcd(R, n_ctx_rows, h_row_offset), ROW_TILE, 16)
    nct = n_ctx_rows // tr
    off = h_row_offset // tr
    cond = lambda b, r: (jnp.where(r < nct, B, b), 0, 0)
    row = pl.BlockSpec((1, tr, D), lambda b, r: (b, r, 0))
    vec = pl.BlockSpec((1, D), lambda b, r: (0, 0))
    return pl.pallas_call(
        functools.partial(_resid_ln_kernel, gate=gate, shift=shift, scale=scale),
        out_shape=(jax.ShapeDtypeStruct((B, R, D), F32), jax.ShapeDtypeStruct((B, R, D), BF16)),
        grid=(B, R // tr),
        in_specs=[pl.BlockSpec((1, tr, D), lambda b, r: (b, r + off, 0)), row,
                  pl.BlockSpec((1, 6, D), cond), pl.BlockSpec((1, 6, D), cond), vec, vec],
        out_specs=(row, row),
        compiler_params=_params(("parallel", "arbitrary"), 48 << 20),
        name="resid_ln",
    )(h, y, mods, mods_next, ln_g.reshape(1, D), ln_b.reshape(1, D))


def _gates_kernel(w_ref, a_ref, b_ref, o_ref):
    g = lax.dot_general(w_ref[...].astype(BF16), a_ref[0], (((1,), (1,)), ((), ())),
                        preferred_element_type=F32)
    o_ref[0] = g + b_ref[...]


def _mlstm_gates(a, w_gate_t, gate_b):
    B, S, D = a.shape
    G = w_gate_t.shape[0]
    ts = _divisor(S, 1088, LANES)
    return pl.pallas_call(
        _gates_kernel,
        out_shape=jax.ShapeDtypeStruct((B, G, S), F32),
        grid=(B, S // ts),
        in_specs=[pl.BlockSpec((G, D), lambda b, s: (0, 0)),
                  pl.BlockSpec((1, ts, D), lambda b, s: (b, s, 0)),
                  pl.BlockSpec((G, 1), lambda b, s: (0, 0))],
        out_specs=pl.BlockSpec((1, G, ts), lambda b, s: (b, 0, s)),
        compiler_params=_params(("parallel", "arbitrary"), 0),
        name="mlstm_gates",
    )(w_gate_t, a, gate_b.reshape(G, 1))


def _log_sigmoid(x):
    return jnp.minimum(x, 0.0) - jnp.log1p(jnp.exp(-jnp.abs(x)))


def _mlstm_scan_kernel(g_ref, q_ref, k_ref, v_ref, h_ref, c_ref, n_ref, m_ref, *, heads, dk, dv, chunk):
    direction = pl.program_id(1)
    step = pl.program_id(2)
    L = chunk

    @pl.when(step == 0)
    def _():
        c_ref[...] = jnp.zeros_like(c_ref)
        n_ref[...] = jnp.zeros_like(n_ref)
        m_ref[...] = jnp.zeros_like(m_ref)

    row = lax.broadcasted_iota(jnp.int32, (L, L), 0)
    col = lax.broadcasted_iota(jnp.int32, (L, L), 1)
    seen = jnp.where(direction == 0, col - row, row - col) <= 0
    eye = row == col

    def to_col(r):
        return jnp.sum(jnp.where(eye, r, 0.0), axis=1, keepdims=True)

    def to_row(c):
        return jnp.sum(jnp.where(eye, c, 0.0), axis=0, keepdims=True)

    for h in range(heads):
        ig = g_ref[0, pl.ds(direction * 2 * heads + h, 1), :]
        lf = _log_sigmoid(g_ref[0, pl.ds(direction * 2 * heads + heads + h, 1), :])
        q = q_ref[0, :, h * dk:(h + 1) * dk] * (dk ** -0.5)
        k = k_ref[0, :, h * dk:(h + 1) * dk]
        v = v_ref[0, :, h * dv:(h + 1) * dv]
        m_prev = m_ref[h]

        b_col = jnp.sum(jnp.where(seen, lf, 0.0), axis=1, keepdims=True)
        b_row = to_row(b_col)
        d = jnp.where(seen, b_col - b_row + ig, MASKED)
        inter = b_col + m_prev
        m_t = jnp.maximum(inter, jnp.max(d, axis=1, keepdims=True))
        s = lax.dot_general(q, k, (((1,), (1,)), ((), ())), preferred_element_type=F32) * jnp.exp(d - m_t)
        g_in = jnp.exp(inter - m_t)
        num = (g_in * jnp.dot(q, c_ref[h].astype(BF16), preferred_element_type=F32)
               + jnp.dot(s.astype(BF16), v, preferred_element_type=F32))
        qf = q.astype(F32)
        den = g_in * jnp.sum(qf * n_ref[h], axis=1, keepdims=True) + jnp.sum(s, axis=1, keepdims=True)
        h_ref[0, 0, :, h * dv:(h + 1) * dv] = num / jnp.maximum(jnp.abs(den), jnp.exp(-m_t))

        b_last = jnp.sum(lf, axis=1, keepdims=True)
        w_row = b_last - b_row + ig
        m_new = jnp.maximum(b_last + m_prev, jnp.max(w_row, axis=1, keepdims=True))
        decay = jnp.exp(b_last + m_prev - m_new)
        kw = k.astype(F32) * to_col(jnp.exp(w_row - m_new))
        c_ref[h] = decay * c_ref[h] + jnp.dot(kw.T.astype(BF16), v, preferred_element_type=F32)
        n_ref[h] = decay * n_ref[h] + jnp.sum(kw, axis=0, keepdims=True)
        m_ref[h] = m_new


def _mlstm_scan(p, gates, *, n_ctx_rows, heads, qk, vw):
    B, S, _ = p.shape
    L = _divisor(math.gcd(S, n_ctx_rows), ML_CHUNK, LANES)
    nc, ncc = S // L, n_ctx_rows // L
    dk, dv = qk // heads, vw // heads

    def chunk(d, j):
        back = jnp.where(j < ncc, ncc - 1 - j, nc - 1 - (j - ncc))
        return jnp.where(d == 0, j, back)

    vmem = (2 * (2 * _nbytes((L, qk), BF16) + _nbytes((L, vw), BF16) + _nbytes((L, vw), F32))
            + _nbytes((heads, dk, dv), F32) + (16 << 20))
    return pl.pallas_call(
        functools.partial(_mlstm_scan_kernel, heads=heads, dk=dk, dv=dv, chunk=L),
        out_shape=jax.ShapeDtypeStruct((2, B, S, vw), F32),
        grid=(B, 2, nc),
        in_specs=[pl.BlockSpec((1, 4 * heads, L), lambda b, d, j: (b, 0, chunk(d, j))),
                  pl.BlockSpec((1, L, qk), lambda b, d, j: (b, chunk(d, j), 0)),
                  pl.BlockSpec((1, L, qk), lambda b, d, j: (b, chunk(d, j), 1)),
                  pl.BlockSpec((1, L, vw), lambda b, d, j: (b, chunk(d, j), (2 * qk) // vw))],
        out_specs=pl.BlockSpec((1, 1, L, vw), lambda b, d, j: (d, b, chunk(d, j), 0)),
        scratch_shapes=[pltpu.VMEM((heads, dk, dv), F32), pltpu.VMEM((heads, 1, dk), F32),
                        pltpu.VMEM((heads, 1, 1), F32)],
        compiler_params=_params(("parallel", "parallel", "arbitrary"), vmem),
        name="mlstm_scan",
    )(gates, p, p, p)


def _mlstm_norm_kernel(h_ref, o_ref, g_ref, out_ref, *, heads, dv):
    hs = h_ref[0, 0] + h_ref[1, 0]
    for h in range(heads):
        x = hs[:, h * dv:(h + 1) * dv]
        x = x * lax.rsqrt(jnp.mean(x * x, axis=-1, keepdims=True) + RMS_EPS)
        x = x * g_ref[:, h * dv:(h + 1) * dv] * jax.nn.sigmoid(o_ref[0, :, h * dv:(h + 1) * dv].astype(F32))
        out_ref[0, :, h * dv:(h + 1) * dv] = x.astype(out_ref.dtype)


def _mlstm_norm(hfb, p, norm_g, *, heads, o_col_block):
    _, B, S, vw = hfb.shape
    tr = _divisor(S, ROW_TILE, 16)
    return pl.pallas_call(
        functools.partial(_mlstm_norm_kernel, heads=heads, dv=vw // heads),
        out_shape=jax.ShapeDtypeStruct((B, S, vw), BF16),
        grid=(B, S // tr),
        in_specs=[pl.BlockSpec((2, 1, tr, vw), lambda b, r: (0, b, r, 0)),
                  pl.BlockSpec((1, tr, vw), lambda b, r: (b, r, o_col_block)),
                  pl.BlockSpec((1, vw), lambda b, r: (0, 0))],
        out_specs=pl.BlockSpec((1, tr, vw), lambda b, r: (b, r, 0)),
        compiler_params=_params(("parallel", "arbitrary"), 48 << 20),
        name="mlstm_norm",
    )(hfb, p, norm_g.reshape(1, vw))


def _qkv_rope_kernel(a_ref, w_ref, cos_ref, sin_ref, o_ref, *, rope_blocks):
    j = pl.program_id(1)
    acc = jnp.dot(a_ref[...], w_ref[...].astype(BF16), preferred_element_type=F32)

    @pl.when(j < rope_blocks)
    def _():
        tn = acc.shape[1]
        lane = lax.broadcasted_iota(jnp.int32, (1, LANES), 1)
        first = (lane // (LANES // 4)) % 2 == 0
        for hd in range(tn // LANES):
            x = acc[:, hd * LANES:(hd + 1) * LANES]
            partner = jnp.where(first, pltpu.roll(x, LANES - LANES // 4, 1), pltpu.roll(x, LANES // 4, 1))
            o_ref[:, hd * LANES:(hd + 1) * LANES] = (x * cos_ref[...] + partner * sin_ref[...]).astype(o_ref.dtype)

    @pl.when(j >= rope_blocks)
    def _():
        o_ref[...] = acc.astype(o_ref.dtype)


def _qkv_rope(a, w, cos, sin, *, rope_cols, rows_per_batch, tn=512):
    M, K = a.shape
    N = w.shape[1]
    tm = _divisor(rows_per_batch, MM_ROWS, 16)
    tn = _divisor(math.gcd(N, rope_cols), tn, LANES)
    ntab = rows_per_batch // tm
    vmem = (2 * _nbytes((tm, K), BF16) + 2 * _nbytes((K, tn), F32) + _nbytes((K, tn), BF16)
            + 2 * _nbytes((tm, tn), BF16) + 3 * _nbytes((tm, tn), F32) + (6 << 20))
    return pl.pallas_call(
        functools.partial(_qkv_rope_kernel, rope_blocks=rope_cols // tn),
        out_shape=jax.ShapeDtypeStruct((M, N), BF16),
        grid=(M // tm, N // tn),
        in_specs=[pl.BlockSpec((tm, K), lambda i, j: (i, 0)),
                  pl.BlockSpec((K, tn), lambda i, j: (0, j)),
                  pl.BlockSpec((tm, LANES), lambda i, j: (i % ntab, 0)),
                  pl.BlockSpec((tm, LANES), lambda i, j: (i % ntab, 0))],
        out_specs=pl.BlockSpec((tm, tn), lambda i, j: (i, j)),
        compiler_params=_params(("parallel", "arbitrary"), vmem),
        name="qkv_rope",
    )(a, w, cos, sin)


def _attention_kernel(sink_ref, q_ref, kp_ref, kc_ref, kn_ref, kx_ref, vp_ref, vc_ref, vn_ref, vx_ref, o_ref,
                      *, group, n_blocks, scale):
    kv = pl.program_id(1)
    n = pl.program_id(2)
    T = q_ref.shape[1]
    row = lax.broadcasted_iota(jnp.int32, (T, T), 0)
    col = lax.broadcasted_iota(jnp.int32, (T, T), 1)
    prev_ok = (col >= row) & (n > 0)
    next_ok = (col <= row) & (n < n_blocks - 1)
    nt = (((1,), (1,)), ((), ()))
    for g in range(group):
        q = q_ref[0, :, g * LANES:(g + 1) * LANES]
        s_p = jnp.where(prev_ok, lax.dot_general(q, kp_ref[0], nt, preferred_element_type=F32) * scale, MASKED)
        s_c = lax.dot_general(q, kc_ref[0], nt, preferred_element_type=F32) * scale
        s_n = jnp.where(next_ok, lax.dot_general(q, kn_ref[0], nt, preferred_element_type=F32) * scale, MASKED)
        s_x = lax.dot_general(q, kx_ref[0], nt, preferred_element_type=F32) * scale
        sink = sink_ref[kv * group + g]
        m = jnp.maximum(jnp.maximum(jnp.max(s_p, axis=1, keepdims=True), jnp.max(s_c, axis=1, keepdims=True)),
                        jnp.maximum(jnp.max(s_n, axis=1, keepdims=True), jnp.max(s_x, axis=1, keepdims=True)))
        m = jnp.maximum(m, sink)
        e_p, e_c, e_n, e_x = jnp.exp(s_p - m), jnp.exp(s_c - m), jnp.exp(s_n - m), jnp.exp(s_x - m)
        total = (jnp.sum(e_p, axis=1, keepdims=True) + jnp.sum(e_c, axis=1, keepdims=True)
                 + jnp.sum(e_n, axis=1, keepdims=True) + jnp.sum(e_x, axis=1, keepdims=True) + jnp.exp(sink - m))
        inv = 1.0 / total
        o = (jnp.dot((e_p * inv).astype(BF16), vp_ref[0], preferred_element_type=F32)
             + jnp.dot((e_c * inv).astype(BF16), vc_ref[0], preferred_element_type=F32)
             + jnp.dot((e_n * inv).astype(BF16), vn_ref[0], preferred_element_type=F32)
             + jnp.dot((e_x * inv).astype(BF16), vx_ref[0], preferred_element_type=F32))
        o_ref[0, :, g * LANES:(g + 1) * LANES] = o.astype(o_ref.dtype)


def _window_attention(p, sink, *, n_ctx_rows, heads, kv_heads):
    B, S, _ = p.shape
    T = S - n_ctx_rows
    assert WINDOW == Q_BLOCK and T % Q_BLOCK == 0 and n_ctx_rows % Q_BLOCK == 0
    nb = T // Q_BLOCK
    cb = n_ctx_rows // Q_BLOCK
    group = heads // kv_heads
    kcol, vcol = heads, heads + kv_heads

    def blk(colbase, shift):
        def index(b, kv, n):
            return (b, jnp.clip(n + shift, 0, nb - 1) + cb, colbase + kv)
        return pl.BlockSpec((1, Q_BLOCK, LANES), index)

    def ctx(colbase):
        return pl.BlockSpec((1, n_ctx_rows, LANES), lambda b, kv, n: (b, 0, colbase + kv))

    return pl.pallas_call(
        functools.partial(_attention_kernel, group=group, n_blocks=nb, scale=LANES ** -0.5),
        out_shape=jax.ShapeDtypeStruct((B, T, heads * LANES), BF16),
        grid=(B, kv_heads, nb),
        in_specs=[pl.BlockSpec(memory_space=pltpu.SMEM),
                  pl.BlockSpec((1, Q_BLOCK, group * LANES), lambda b, kv, n: (b, n + cb, kv)),
                  blk(kcol, -1), blk(kcol, 0), blk(kcol, 1), ctx(kcol),
                  blk(vcol, -1), blk(vcol, 0), blk(vcol, 1), ctx(vcol)],
        out_specs=pl.BlockSpec((1, Q_BLOCK, group * LANES), lambda b, kv, n: (b, n, kv)),
        compiler_params=_params(("parallel", "parallel", "arbitrary"), 0),
        name="window_attention",
    )(sink, p, p, p, p, p, p, p, p, p)


def _rope_tables(n_ctx_rows, seq):
    quarter = LANES // 4
    freqs = ROPE_THETA ** (-jnp.arange(quarter, dtype=F32) / quarter)
    t = jnp.arange(seq)
    ang_r = (t // GRID_W).astype(F32)[:, None] * freqs[None, :]
    ang_c = (t % GRID_W).astype(F32)[:, None] * freqs[None, :]
    cos = jnp.concatenate([jnp.cos(ang_r), jnp.cos(ang_r), jnp.cos(ang_c), jnp.cos(ang_c)], axis=-1)
    sin = jnp.concatenate([-jnp.sin(ang_r), jnp.sin(ang_r), -jnp.sin(ang_c), jnp.sin(ang_c)], axis=-1)
    cos = jnp.concatenate([jnp.ones((n_ctx_rows, LANES), F32), cos], axis=0)
    sin = jnp.concatenate([jnp.zeros((n_ctx_rows, LANES), F32), sin], axis=0)
    return cos, sin


def _router_kernel(w_ref, f_ref, idx_ref, wt_ref):
    logits = lax.dot_general(w_ref[...].astype(BF16), f_ref[...], (((1,), (1,)), ((), ())),
                             preferred_element_type=F32)
    n_exp = logits.shape[0]
    eid = lax.broadcasted_iota(jnp.int32, logits.shape, 0)
    v1 = jnp.max(logits, axis=0, keepdims=True)
    i1 = jnp.min(jnp.where(logits == v1, eid, n_exp), axis=0, keepdims=True)
    rest = jnp.where(eid == i1, -jnp.inf, logits)
    v2 = jnp.max(rest, axis=0, keepdims=True)
    i2 = jnp.min(jnp.where(rest == v2, eid, n_exp), axis=0, keepdims=True)
    e2 = jnp.exp(v2 - v1)
    idx_ref[0:1, :] = i1
    idx_ref[1:2, :] = i2
    wt_ref[0:1, :] = 1.0 / (1.0 + e2)
    wt_ref[1:2, :] = e2 / (1.0 + e2)


def _router(f, w_router_t):
    T, D = f.shape
    E = w_router_t.shape[0]
    tt = _divisor(T, 1024, LANES)
    return pl.pallas_call(
        _router_kernel,
        out_shape=(jax.ShapeDtypeStruct((TOP_K, T), jnp.int32), jax.ShapeDtypeStruct((TOP_K, T), F32)),
        grid=(T // tt,),
        in_specs=[pl.BlockSpec((E, D), lambda i: (0, 0)), pl.BlockSpec((tt, D), lambda i: (i, 0))],
        out_specs=(pl.BlockSpec((TOP_K, tt), lambda i: (0, i)), pl.BlockSpec((TOP_K, tt), lambda i: (0, i))),
        compiler_params=_params(("arbitrary",), 0),
        name="moe_router",
    )(w_router_t, f)


GATHER_ROWS = 256


def _gather_kernel(idx_ref, src_ref, dst_ref, sem):
    base = pl.program_id(0) * GATHER_ROWS

    def issue(r, carry):
        pltpu.make_async_copy(src_ref.at[idx_ref[base + r]], dst_ref.at[base + r], sem).start()
        return carry

    lax.fori_loop(0, GATHER_ROWS, issue, 0)

    def drain(r, carry):
        pltpu.make_async_copy(src_ref.at[0], dst_ref.at[base + r], sem).wait()
        return carry

    lax.fori_loop(0, GATHER_ROWS, drain, 0)


def _gather_rows(src, idx):
    N, D = src.shape
    R = idx.shape[0]
    assert R % GATHER_ROWS == 0 and D % LANES == 0
    out = pl.pallas_call(
        _gather_kernel,
        out_shape=jax.ShapeDtypeStruct((R, D // LANES, LANES), src.dtype),
        grid_spec=pltpu.PrefetchScalarGridSpec(
            num_scalar_prefetch=1,
            grid=(R // GATHER_ROWS,),
            in_specs=[pl.BlockSpec(memory_space=pl.ANY)],
            out_specs=pl.BlockSpec(memory_space=pl.ANY),
            scratch_shapes=[pltpu.SemaphoreType.DMA(())],
        ),
        compiler_params=pltpu.CompilerParams(dimension_semantics=("arbitrary",)),
        name="gather_rows",
    )(idx, src.reshape(N, D // LANES, LANES))
    return out.reshape(R, D)


def _expert_up_kernel(te_ref, tv_ref, x_ref, wg_ref, wu_ref, o_ref, wgb_ref, wub_ref):
    i = pl.program_id(1)
    fresh = jnp.logical_or(i == 0, te_ref[i] != te_ref[jnp.maximum(i - 1, 0)])

    @pl.when(fresh)
    def _():
        wgb_ref[...] = wg_ref[0].astype(BF16)
        wub_ref[...] = wu_ref[0].astype(BF16)

    @pl.when(tv_ref[i] != 0)
    def _():
        x = x_ref[...]
        g = jnp.dot(x, wgb_ref[...], preferred_element_type=F32)
        u = jnp.dot(x, wub_ref[...], preferred_element_type=F32)
        o_ref[...] = (g * jax.nn.sigmoid(g) * u).astype(o_ref.dtype)

    @pl.when(tv_ref[i] == 0)
    def _():
        o_ref[...] = jnp.zeros_like(o_ref)


def _expert_up(xs, w_gu, tile_expert, tile_valid, *, tn=256):
    P, D = xs.shape
    E, _, F2 = w_gu.shape
    F = F2 // 2
    tm = MOE_ROWS
    tn = _divisor(F, tn, LANES)
    nf = F // tn
    vmem = (2 * _nbytes((tm, D), BF16) + 4 * _nbytes((D, tn), F32) + 2 * _nbytes((D, tn), BF16)
            + 2 * _nbytes((tm, tn), BF16) + 3 * _nbytes((tm, tn), F32) + (4 << 20))
    return pl.pallas_call(
        _expert_up_kernel,
        out_shape=jax.ShapeDtypeStruct((P, F), BF16),
        grid_spec=pltpu.PrefetchScalarGridSpec(
            num_scalar_prefetch=2,
            grid=(nf, P // tm),
            in_specs=[pl.BlockSpec((tm, D), lambda j, i, te, tv: (i, 0)),
                      pl.BlockSpec((1, D, tn), lambda j, i, te, tv: (te[i], 0, j)),
                      pl.BlockSpec((1, D, tn), lambda j, i, te, tv: (te[i], 0, j + nf))],
            out_specs=pl.BlockSpec((tm, tn), lambda j, i, te, tv: (i, j)),
            scratch_shapes=[pltpu.VMEM((D, tn), BF16), pltpu.VMEM((D, tn), BF16)],
        ),
        compiler_params=_params(("arbitrary", "arbitrary"), vmem),
        name="expert_up",
    )(tile_expert, tile_valid, xs, w_gu, w_gu)


def _expert_down_kernel(te_ref, tv_ref, a_ref, w_ref, gw_ref, o_ref, wb_ref):
    i = pl.program_id(1)
    fresh = jnp.logical_or(i == 0, te_ref[i] != te_ref[jnp.maximum(i - 1, 0)])

    @pl.when(fresh)
    def _():
        wb_ref[...] = w_ref[0].astype(BF16)

    @pl.when(tv_ref[i] != 0)
    def _():
        o_ref[...] = gw_ref[...] * jnp.dot(a_ref[...], wb_ref[...], preferred_element_type=F32)

    @pl.when(tv_ref[i] == 0)
    def _():
        o_ref[...] = jnp.zeros_like(o_ref)


def _expert_down(act, w_down, row_gate, tile_expert, tile_valid, *, tn=512):
    P, F = act.shape
    E, _, D = w_down.shape
    tm = MOE_ROWS
    tn = _divisor(D, tn, LANES)
    vmem = (2 * _nbytes((tm, F), BF16) + 2 * _nbytes((F, tn), F32) + _nbytes((F, tn), BF16)
            + 3 * _nbytes((tm, tn), F32) + (4 << 20))
    return pl.pallas_call(
        _expert_down_kernel,
        out_shape=jax.ShapeDtypeStruct((P, D), F32),
        grid_spec=pltpu.PrefetchScalarGridSpec(
            num_scalar_prefetch=2,
            grid=(D // tn, P // tm),
            in_specs=[pl.BlockSpec((tm, F), lambda j, i, te, tv: (i, 0)),
                      pl.BlockSpec((1, F, tn), lambda j, i, te, tv: (te[i], 0, j)),
                      pl.BlockSpec((tm, 1), lambda j, i, te, tv: (i, 0))],
            out_specs=pl.BlockSpec((tm, tn), lambda j, i, te, tv: (i, j)),
            scratch_shapes=[pltpu.VMEM((F, tn), BF16)],
        ),
        compiler_params=_params(("arbitrary", "arbitrary"), vmem),
        name="expert_down",
    )(tile_expert, tile_valid, act, w_down, row_gate.reshape(P, 1))


def _routing_tables(idx, wts, n_experts):
    K, T = idx.shape
    tm = MOE_ROWS
    n_tiles = (K * T) // tm + n_experts
    P = n_tiles * tm
    e = idx.reshape(-1)
    onehot = (e[:, None] == jnp.arange(n_experts)[None, :]).astype(jnp.int32)
    csum = jnp.cumsum(onehot, axis=0)
    rank = jnp.take_along_axis(csum, e[:, None], axis=1)[:, 0] - 1
    counts = csum[-1]
    tiles_per = (counts + tm - 1) // tm
    tile_end = jnp.cumsum(tiles_per)
    start = (tile_end - tiles_per) * tm
    slot = start[e] + rank
    token = jnp.tile(jnp.arange(T, dtype=jnp.int32), K)
    slot_token = jnp.zeros((P,), jnp.int32).at[slot].set(token)
    slot_gate = jnp.zeros((P,), F32).at[slot].set(wts.reshape(-1))
    tiles = jnp.arange(n_tiles, dtype=jnp.int32)
    tile_valid = (tiles < tile_end[-1]).astype(jnp.int32)
    owner = jnp.searchsorted(tile_end, tiles, side="right").astype(jnp.int32)
    last_owner = jnp.searchsorted(tile_end, tile_end[-1] - 1, side="right").astype(jnp.int32)
    tile_expert = jnp.where(tile_valid != 0, owner, last_owner)
    return slot_token, slot_gate, slot.reshape(K, T).astype(jnp.int32), tile_expert, tile_valid


def _final_ln_kernel(h_ref, y_ref, m_ref, g_ref, b_ref, o_ref, *, gate):
    m = m_ref[0]
    y = y_ref[0, 0] + y_ref[1, 0]
    o_ref[0] = _layer_norm_rows(ALPHA * h_ref[0] + m[gate:gate + 1] * y, g_ref[...], b_ref[...])


def _final_ln(h, y2, mods, ln_g, ln_b, *, gate):
    B, T, D = h.shape
    tr = _divisor(T, ROW_TILE, 16)
    row = pl.BlockSpec((1, tr, D), lambda b, r: (b, r, 0))
    vec = pl.BlockSpec((1, D), lambda b, r: (0, 0))
    return pl.pallas_call(
        functools.partial(_final_ln_kernel, gate=gate),
        out_shape=jax.ShapeDtypeStruct((B, T, D), F32),
        grid=(B, T // tr),
        in_specs=[row, pl.BlockSpec((2, 1, tr, D), lambda b, r: (0, b, r, 0)),
                  pl.BlockSpec((1, 6, D), lambda b, r: (b, 0, 0)), vec, vec],
        out_specs=row,
        compiler_params=_params(("parallel", "arbitrary"), 48 << 20),
        name="final_ln",
    )(h, y2, mods, ln_g.reshape(1, D), ln_b.reshape(1, D))


SHIFT_MIX, SCALE_MIX, GATE_MIX, SHIFT_FFN, SCALE_FFN, GATE_FFN = range(6)


def kernel(x, c, ctx, c_ctx, ada_w, ada_b, ln_g, ln_b, ml_w_in, ml_gate_b, ml_norm_g, ml_w_out, att_w_in, att_sink,
           att_w_out, ffn_w_gu, ffn_w_down, moe_router, moe_w_gu, moe_w_down):
    B, T, D = x.shape
    C = ctx.shape[1]
    S = C + T
    assert ada_w.shape[0] == DEPTH == 2
    qk, vw = ml_w_out.shape[1] // 2, ml_w_out.shape[1]
    n_gates = 4 * ML_HEADS
    n_experts = moe_router.shape[-1]

    cond = jnp.concatenate([c, c_ctx[None, :]], axis=0)
    mods = [_adaln(cond, ada_w[i], ada_b[i]) for i in range(DEPTH)]

    h0 = jnp.concatenate([ctx, x], axis=1)

    a = _modulate(h0, mods[0], n_ctx_rows=C, shift=SHIFT_MIX, scale=SCALE_MIX)
    w_in = ml_w_in[0]
    n_main = 2 * qk + 2 * vw
    p = _matmul(a.reshape(B * S, D), w_in, n_out=n_main, out_dtype=BF16, name="mlstm_in").reshape(B, S, n_main)
    gates = _mlstm_gates(a, w_in[:, n_main:].T, ml_gate_b[0])
    hfb = _mlstm_scan(p, gates, n_ctx_rows=C, heads=ML_HEADS, qk=qk, vw=vw)
    hn = _mlstm_norm(hfb, p, ml_norm_g[0], heads=ML_HEADS, o_col_block=(2 * qk + vw) // vw)
    y = _matmul(hn.reshape(B * S, vw), ml_w_out[0], name="mlstm_out").reshape(B, S, D)
    h1, f = _resid_ln(h0, y, mods[0], mods[0], ln_g[0, 0], ln_b[0, 0], n_ctx_rows=C, h_row_offset=0,
                      gate=GATE_MIX, shift=SHIFT_FFN, scale=SCALE_FFN)
    act = _swiglu_up(f.reshape(B * S, D), ffn_w_gu[0])
    d_ff = act.shape[1]
    tk = d_ff // 2 if (d_ff // 2) % LANES == 0 else d_ff
    y = _matmul_ksplit(act, ffn_w_down[0], tk=tk, name="ffn_down").reshape(B, S, D)
    h2, a = _resid_ln(h1, y, mods[0], mods[1], ln_g[0, 1], ln_b[0, 1], n_ctx_rows=C, h_row_offset=0,
                      gate=GATE_FFN, shift=SHIFT_MIX, scale=SCALE_MIX)

    q_cols = ATT_HEADS * LANES
    cos, sin = _rope_tables(C, T)
    p = _qkv_rope(a.reshape(B * S, D), att_w_in[0], cos, sin, rope_cols=q_cols + ATT_KV * LANES,
                  rows_per_batch=S).reshape(B, S, -1)
    o = _window_attention(p, att_sink[0], n_ctx_rows=C, heads=ATT_HEADS, kv_heads=ATT_KV)
    y = _matmul(o.reshape(B * T, q_cols), att_w_out[0], name="att_out").reshape(B, T, D)
    h3, f = _resid_ln(h2, y, mods[1], mods[1], ln_g[1, 0], ln_b[1, 0], n_ctx_rows=0, h_row_offset=C,
                      gate=GATE_MIX, shift=SHIFT_FFN, scale=SCALE_FFN)

    f2 = f.reshape(B * T, D)
    idx, wts = _router(f2, moe_router[0].T)
    slot_token, slot_gate, pair_slot, tile_expert, tile_valid = _routing_tables(idx, wts, n_experts)
    xs = _gather_rows(f2, slot_token)
    act = _expert_up(xs, moe_w_gu[0], tile_expert, tile_valid)
    ys = _expert_down(act, moe_w_down[0], slot_gate, tile_expert, tile_valid)
    y2 = _gather_rows(ys, pair_slot.reshape(-1)).reshape(TOP_K, B, T, D)
    return _final_ln(h3, y2, mods[1], ln_g[1, 1], ln_b[1, 1], gate=GATE_FFN)
```

```python
import functools
import math

import jax
import jax.numpy as jnp
from jax import lax
from jax.experimental import pallas as pl
from jax.experimental.pallas import tpu as pltpu

F32 = jnp.float32
BF16 = jnp.bfloat16

GRID_W = 64
ML_HEADS = 8
ATT_HEADS = 32
ATT_KV = 8
WINDOW = 128
Q_BLOCK = 128
ROPE_THETA = 10000.0
TOP_K = 2
LN_EPS = 1e-5
RMS_EPS = 1e-6
DEPTH = 2
ALPHA = (2 * DEPTH) ** 0.25

V7X_VMEM_BYTES = 64 * 1024 * 1024
VMEM_CAP = V7X_VMEM_BYTES - 8 * 1024 * 1024
LANES = 128
MASKED = -1e30

ROW_TILE = 256
MM_ROWS = 1088
ML_CHUNK = 256
MOE_ROWS = 512


def _divisor(n, target, mult):
    best = None
    for d in range(mult, min(n, target) + 1, mult):
        if n % d == 0:
            best = d
    return best if best is not None else n


def _params(sem, vmem_bytes):
    limit = int(min(VMEM_CAP, max(32 * 1024 * 1024, vmem_bytes)))
    return pltpu.CompilerParams(dimension_semantics=sem, vmem_limit_bytes=limit)


def _nbytes(shape, dtype):
    return math.prod(shape) * jnp.dtype(dtype).itemsize


def _mm_kernel(a_ref, w_ref, o_ref):
    o_ref[...] = jnp.dot(a_ref[...], w_ref[...].astype(BF16), preferred_element_type=F32).astype(o_ref.dtype)


def _matmul(a, w, layer, *, n_out=None, out_dtype=F32, tn=512, a_buffers=2, name="matmul"):
    M, K = a.shape
    n_out = w.shape[2] if n_out is None else n_out
    tm = _divisor(M, MM_ROWS, 16)
    tn = _divisor(n_out, tn, LANES)
    vmem = (a_buffers * _nbytes((tm, K), BF16) + 2 * _nbytes((K, tn), w.dtype) + _nbytes((K, tn), BF16)
            + 2 * _nbytes((tm, tn), out_dtype) + _nbytes((tm, tn), F32) + (4 << 20))
    a_mode = {} if a_buffers == 2 else {"pipeline_mode": pl.Buffered(a_buffers)}
    return pl.pallas_call(
        _mm_kernel,
        out_shape=jax.ShapeDtypeStruct((M, n_out), out_dtype),
        grid=(M // tm, n_out // tn),
        in_specs=[pl.BlockSpec((tm, K), lambda i, j: (i, 0), **a_mode),
                  pl.BlockSpec((None, K, tn), lambda i, j: (layer, 0, j))],
        out_specs=pl.BlockSpec((tm, tn), lambda i, j: (i, j)),
        compiler_params=_params(("parallel", "arbitrary"), vmem),
        name=name,
    )(a, w)


def _cast_kernel(x_ref, o_ref):
    o_ref[...] = x_ref[...].astype(o_ref.dtype)


def _cast_bf16(w):
    L, K, N = w.shape
    tk = _divisor(K, 512, 16)
    return pl.pallas_call(
        _cast_kernel,
        out_shape=jax.ShapeDtypeStruct((L, K, N), BF16),
        grid=(L, K // tk),
        in_specs=[pl.BlockSpec((None, tk, N), lambda l, k: (l, k, 0))],
        out_specs=pl.BlockSpec((None, tk, N), lambda l, k: (l, k, 0)),
        compiler_params=_params(("parallel", "arbitrary"), 48 << 20),
        name="cast_bf16",
    )(w)


def _swiglu_up_kernel(a_ref, wg_ref, wu_ref, o_ref):
    a = a_ref[...]
    g = jnp.dot(a, wg_ref[...].astype(BF16), preferred_element_type=F32)
    u = jnp.dot(a, wu_ref[...].astype(BF16), preferred_element_type=F32)
    o_ref[...] = (g * jax.nn.sigmoid(g) * u).astype(o_ref.dtype)


def _swiglu_up(a, w_gu, layer, *, tn=256, name="swiglu_up"):
    M, K = a.shape
    F = w_gu.shape[2] // 2
    tm = _divisor(M, MM_ROWS, 16)
    tn = _divisor(F, tn, LANES)
    nf = F // tn
    vmem = (2 * _nbytes((tm, K), BF16) + 4 * _nbytes((K, tn), F32) + 2 * _nbytes((K, tn), BF16)
            + 2 * _nbytes((tm, tn), BF16) + 3 * _nbytes((tm, tn), F32) + (4 << 20))
    return pl.pallas_call(
        _swiglu_up_kernel,
        out_shape=jax.ShapeDtypeStruct((M, F), BF16),
        grid=(M // tm, nf),
        in_specs=[pl.BlockSpec((tm, K), lambda i, j: (i, 0)),
                  pl.BlockSpec((None, K, tn), lambda i, j: (layer, 0, j)),
                  pl.BlockSpec((None, K, tn), lambda i, j: (layer, 0, j + nf))],
        out_specs=pl.BlockSpec((tm, tn), lambda i, j: (i, j)),
        compiler_params=_params(("parallel", "arbitrary"), vmem),
        name=name,
    )(a, w_gu, w_gu)


def _adaln_kernel(c_ref, w_ref, b_ref, o_ref):
    c = c_ref[...]
    a = (c * jax.nn.sigmoid(c)).astype(BF16)
    o_ref[...] = jnp.dot(a, w_ref[...].astype(BF16), preferred_element_type=F32) + b_ref[...]


def _adaln(cond, w, b, layer, *, tn=512):
    R, D = cond.shape
    N = w.shape[2]
    tn = _divisor(N, tn, LANES)
    vmem = 2 * _nbytes((D, tn), F32) + _nbytes((D, tn), BF16) + (4 << 20)
    out = pl.pallas_call(
        _adaln_kernel,
        out_shape=jax.ShapeDtypeStruct((R, N), F32),
        grid=(N // tn,),
        in_specs=[pl.BlockSpec((R, D), lambda j: (0, 0)),
                  pl.BlockSpec((None, D, tn), lambda j: (layer, 0, j)),
                  pl.BlockSpec((None, 1, tn), lambda j: (layer, 0, j))],
        out_specs=pl.BlockSpec((R, tn), lambda j: (0, j)),
        compiler_params=_params(("arbitrary",), vmem),
        name="adaln",
    )(cond, w, b.reshape(b.shape[0], 1, N))
    return out.reshape(R, 6, D)


def _modulate_kernel(h_ref, m_ref, o_ref, *, shift, scale):
    m = m_ref[0]
    o_ref[0] = (h_ref[0] * (1.0 + m[scale:scale + 1]) + m[shift:shift + 1]).astype(o_ref.dtype)


def _modulate(h, mods, *, n_ctx_rows, shift, scale):
    B, S, D = h.shape
    tr = _divisor(math.gcd(S, n_ctx_rows), ROW_TILE, 16)
    nct = n_ctx_rows // tr
    return pl.pallas_call(
        functools.partial(_modulate_kernel, shift=shift, scale=scale),
        out_shape=jax.ShapeDtypeStruct((B, S, D), BF16),
        grid=(B, S // tr),
        in_specs=[pl.BlockSpec((1, tr, D), lambda b, r: (b, r, 0)),
                  pl.BlockSpec((1, 6, D), lambda b, r: (jnp.where(r < nct, B, b), 0, 0))],
        out_specs=pl.BlockSpec((1, tr, D), lambda b, r: (b, r, 0)),
        compiler_params=_params(("parallel", "arbitrary"), 0),
        name="modulate",
    )(h, mods)


def _layer_norm_rows(z, g, b):
    mu = jnp.mean(z, axis=-1, keepdims=True)
    zc = z - mu
    var = jnp.mean(zc * zc, axis=-1, keepdims=True)
    return zc * lax.rsqrt(var + LN_EPS) * g + b


def _resid_ln_kernel(h_ref, y_ref, m_ref, mn_ref, g_ref, b_ref, hn_ref, f_ref, *, gate, shift, scale):
    m = m_ref[0]
    hn = _layer_norm_rows(ALPHA * h_ref[0] + m[gate:gate + 1] * y_ref[0], g_ref[...], b_ref[...])
    hn_ref[0] = hn
    mn = mn_ref[0]
    f_ref[0] = (hn * (1.0 + mn[scale:scale + 1]) + mn[shift:shift + 1]).astype(f_ref.dtype)


def _resid_ln(h, y, mods, mods_next, ln_g, ln_b, *, n_ctx_rows, h_row_offset, gate, shift, scale, f_dtype=BF16):
    B, R, D = y.shape
    tr = _divisor(math.gcd(R, n_ctx_rows, h_row_offset), ROW_TILE, 16)
    nct = n_ctx_rows // tr
    off = h_row_offset // tr
    cond = lambda b, r: (jnp.where(r < nct, B, b), 0, 0)
    row = pl.BlockSpec((1, tr, D), lambda b, r: (b, r, 0))
    vec = pl.BlockSpec((1, D), lambda b, r: (0, 0))
    return pl.pallas_call(
        functools.partial(_resid_ln_kernel, gate=gate, shift=shift, scale=scale),
        out_shape=(jax.ShapeDtypeStruct((B, R, D), F32), jax.ShapeDtypeStruct((B, R, D), f_dtype)),
        grid=(B, R // tr),
        in_specs=[pl.BlockSpec((1, tr, D), lambda b, r: (b, r + off, 0)), row,
                  pl.BlockSpec((1, 6, D), cond), pl.BlockSpec((1, 6, D), cond), vec, vec],
        out_specs=(row, row),
        compiler_params=_params(("parallel", "arbitrary"), 48 << 20),
        name="resid_ln",
    )(h, y, mods, mods_next, ln_g.reshape(1, D), ln_b.reshape(1, D))


def _gates_kernel(w_ref, a_ref, b_ref, o_ref):
    g = lax.dot_general(w_ref[...].astype(BF16), a_ref[0], (((1,), (1,)), ((), ())),
                        preferred_element_type=F32)
    o_ref[0] = g + b_ref[...]


def _mlstm_gates(a, w_gate_t, gate_b):
    B, S, D = a.shape
    G = w_gate_t.shape[0]
    ts = _divisor(S, 1088, LANES)
    return pl.pallas_call(
        _gates_kernel,
        out_shape=jax.ShapeDtypeStruct((B, G, S), F32),
        grid=(B, S // ts),
        in_specs=[pl.BlockSpec((G, D), lambda b, s: (0, 0)),
                  pl.BlockSpec((1, ts, D), lambda b, s: (b, s, 0)),
                  pl.BlockSpec((G, 1), lambda b, s: (0, 0))],
        out_specs=pl.BlockSpec((1, G, ts), lambda b, s: (b, 0, s)),
        compiler_params=_params(("parallel", "arbitrary"), 0),
        name="mlstm_gates",
    )(w_gate_t, a, gate_b.reshape(G, 1))


def _log_sigmoid(x):
    return jnp.minimum(x, 0.0) - jnp.log1p(jnp.exp(-jnp.abs(x)))


def _mlstm_scan_kernel(g_ref, q_ref, k_ref, v_ref, h_ref, c_ref, n_ref, m_ref, *, heads, dk, dv, chunk):
    direction = pl.program_id(1)
    step = pl.program_id(2)
    L = chunk

    @pl.when(step == 0)
    def _():
        c_ref[...] = jnp.zeros_like(c_ref)
        n_ref[...] = jnp.zeros_like(n_ref)
        m_ref[...] = jnp.zeros_like(m_ref)

    row = lax.broadcasted_iota(jnp.int32, (L, L), 0)
    col = lax.broadcasted_iota(jnp.int32, (L, L), 1)
    seen = jnp.where(direction == 0, col - row, row - col) <= 0
    eye = row == col

    def to_col(r):
        return jnp.sum(jnp.where(eye, r, 0.0), axis=1, keepdims=True)

    def to_row(c):
        return jnp.sum(jnp.where(eye, c, 0.0), axis=0, keepdims=True)

    for h in range(heads):
        ig = g_ref[0, pl.ds(direction * 2 * heads + h, 1), :]
        lf = _log_sigmoid(g_ref[0, pl.ds(direction * 2 * heads + heads + h, 1), :])
        q = q_ref[0, :, h * dk:(h + 1) * dk] * (dk ** -0.5)
        k = k_ref[0, :, h * dk:(h + 1) * dk]
        v = v_ref[0, :, h * dv:(h + 1) * dv]
        m_prev = m_ref[h]

        b_col = jnp.sum(jnp.where(seen, lf, 0.0), axis=1, keepdims=True)
        b_row = to_row(b_col)
        d = jnp.where(seen, b_col - b_row + ig, MASKED)
        inter = b_col + m_prev
        m_t = jnp.maximum(inter, jnp.max(d, axis=1, keepdims=True))
        s = lax.dot_general(q, k, (((1,), (1,)), ((), ())), preferred_element_type=F32) * jnp.exp(d - m_t)
        g_in = jnp.exp(inter - m_t)
        num = (g_in * jnp.dot(q, c_ref[h].astype(BF16), preferred_element_type=F32)
               + jnp.dot(s.astype(BF16), v, preferred_element_type=F32))
        qf = q.astype(F32)
        den = g_in * jnp.sum(qf * n_ref[h], axis=1, keepdims=True) + jnp.sum(s, axis=1, keepdims=True)
        h_ref[0, 0, :, h * dv:(h + 1) * dv] = num / jnp.maximum(jnp.abs(den), jnp.exp(-m_t))

        b_last = jnp.sum(lf, axis=1, keepdims=True)
        w_row = b_last - b_row + ig
        m_new = jnp.maximum(b_last + m_prev, jnp.max(w_row, axis=1, keepdims=True))
        decay = jnp.exp(b_last + m_prev - m_new)
        kw = k.astype(F32) * to_col(jnp.exp(w_row - m_new))
        c_ref[h] = decay * c_ref[h] + jnp.dot(kw.T.astype(BF16), v, preferred_element_type=F32)
        n_ref[h] = decay * n_ref[h] + jnp.sum(kw, axis=0, keepdims=True)
        m_ref[h] = m_new


def _mlstm_scan(p, gates, *, n_ctx_rows, heads, qk, vw):
    B, S, _ = p.shape
    L = _divisor(math.gcd(S, n_ctx_rows), ML_CHUNK, LANES)
    nc, ncc = S // L, n_ctx_rows // L
    dk, dv = qk // heads, vw // heads

    def chunk(d, j):
        back = jnp.where(j < ncc, ncc - 1 - j, nc - 1 - (j - ncc))
        return jnp.where(d == 0, j, back)

    vmem = (2 * (2 * _nbytes((L, qk), BF16) + _nbytes((L, vw), BF16) + _nbytes((L, vw), F32))
            + _nbytes((heads, dk, dv), F32) + (16 << 20))
    return pl.pallas_call(
        functools.partial(_mlstm_scan_kernel, heads=heads, dk=dk, dv=dv, chunk=L),
        out_shape=jax.ShapeDtypeStruct((2, B, S, vw), F32),
        grid=(B, 2, nc),
        in_specs=[pl.BlockSpec((1, 4 * heads, L), lambda b, d, j: (b, 0, chunk(d, j))),
                  pl.BlockSpec((1, L, qk), lambda b, d, j: (b, chunk(d, j), 0)),
                  pl.BlockSpec((1, L, qk), lambda b, d, j: (b, chunk(d, j), 1)),
                  pl.BlockSpec((1, L, vw), lambda b, d, j: (b, chunk(d, j), (2 * qk) // vw))],
        out_specs=pl.BlockSpec((1, 1, L, vw), lambda b, d, j: (d, b, chunk(d, j), 0)),
        scratch_shapes=[pltpu.VMEM((heads, dk, dv), F32), pltpu.VMEM((heads, 1, dk), F32),
                        pltpu.VMEM((heads, 1, 1), F32)],
        compiler_params=_params(("parallel", "parallel", "arbitrary"), vmem),
        name="mlstm_scan",
    )(gates, p, p, p)


def _mlstm_norm_kernel(h_ref, o_ref, g_ref, out_ref, *, heads, dv):
    hs = h_ref[0, 0] + h_ref[1, 0]
    for h in range(heads):
        x = hs[:, h * dv:(h + 1) * dv]
        x = x * lax.rsqrt(jnp.mean(x * x, axis=-1, keepdims=True) + RMS_EPS)
        x = x * g_ref[:, h * dv:(h + 1) * dv] * jax.nn.sigmoid(o_ref[0, :, h * dv:(h + 1) * dv].astype(F32))
        out_ref[0, :, h * dv:(h + 1) * dv] = x.astype(out_ref.dtype)


def _mlstm_norm(hfb, p, norm_g, *, heads, o_col_block):
    _, B, S, vw = hfb.shape
    tr = _divisor(S, ROW_TILE, 16)
    return pl.pallas_call(
        functools.partial(_mlstm_norm_kernel, heads=heads, dv=vw // heads),
        out_shape=jax.ShapeDtypeStruct((B, S, vw), BF16),
        grid=(B, S // tr),
        in_specs=[pl.BlockSpec((2, 1, tr, vw), lambda b, r: (0, b, r, 0)),
                  pl.BlockSpec((1, tr, vw), lambda b, r: (b, r, o_col_block)),
                  pl.BlockSpec((1, vw), lambda b, r: (0, 0))],
        out_specs=pl.BlockSpec((1, tr, vw), lambda b, r: (b, r, 0)),
        compiler_params=_params(("parallel", "arbitrary"), 48 << 20),
        name="mlstm_norm",
    )(hfb, p, norm_g.reshape(1, vw))


def _qkv_rope_kernel(a_ref, w_ref, cos_ref, sin_ref, o_ref, *, rope_blocks):
    j = pl.program_id(1)
    acc = jnp.dot(a_ref[...], w_ref[...].astype(BF16), preferred_element_type=F32)

    @pl.when(j < rope_blocks)
    def _():
        tn = acc.shape[1]
        lane = lax.broadcasted_iota(jnp.int32, (1, LANES), 1)
        first = (lane // (LANES // 4)) % 2 == 0
        for hd in range(tn // LANES):
            x = acc[:, hd * LANES:(hd + 1) * LANES]
            partner = jnp.where(first, pltpu.roll(x, LANES - LANES // 4, 1), pltpu.roll(x, LANES // 4, 1))
            o_ref[:, hd * LANES:(hd + 1) * LANES] = (x * cos_ref[...] + partner * sin_ref[...]).astype(o_ref.dtype)

    @pl.when(j >= rope_blocks)
    def _():
        o_ref[...] = acc.astype(o_ref.dtype)


def _qkv_rope(a, w, layer, cos, sin, *, rope_cols, rows_per_batch, tn=512):
    M, K = a.shape
    N = w.shape[2]
    tm = _divisor(rows_per_batch, MM_ROWS, 16)
    tn = _divisor(math.gcd(N, rope_cols), tn, LANES)
    ntab = rows_per_batch // tm
    vmem = (2 * _nbytes((tm, K), BF16) + 2 * _nbytes((K, tn), F32) + _nbytes((K, tn), BF16)
            + 2 * _nbytes((tm, tn), BF16) + 3 * _nbytes((tm, tn), F32) + (6 << 20))
    return pl.pallas_call(
        functools.partial(_qkv_rope_kernel, rope_blocks=rope_cols // tn),
        out_shape=jax.ShapeDtypeStruct((M, N), BF16),
        grid=(M // tm, N // tn),
        in_specs=[pl.BlockSpec((tm, K), lambda i, j: (i, 0)),
                  pl.BlockSpec((None, K, tn), lambda i, j: (layer, 0, j)),
                  pl.BlockSpec((tm, LANES), lambda i, j: (i % ntab, 0)),
                  pl.BlockSpec((tm, LANES), lambda i, j: (i % ntab, 0))],
        out_specs=pl.BlockSpec((tm, tn), lambda i, j: (i, j)),
        compiler_params=_params(("parallel", "arbitrary"), vmem),
        name="qkv_rope",
    )(a, w, cos, sin)


def _attention_kernel(sink_ref, q_ref, kp_ref, kc_ref, kn_ref, kx_ref, vp_ref, vc_ref, vn_ref, vx_ref, o_ref,
                      *, group, n_blocks, scale):
    kv = pl.program_id(1)
    n = pl.program_id(2)
    T = q_ref.shape[1]
    q = jnp.concatenate([q_ref[0, :, g * LANES:(g + 1) * LANES] for g in range(group)], axis=0)
    k = jnp.concatenate([kp_ref[0], kc_ref[0], kn_ref[0], kx_ref[0]], axis=0)
    v = jnp.concatenate([vp_ref[0], vc_ref[0], vn_ref[0], vx_ref[0]], axis=0)
    s = lax.dot_general(q, k, (((1,), (1,)), ((), ())), preferred_element_type=F32) * scale
    row = lax.broadcasted_iota(jnp.int32, (group * T, T), 0) % T
    col = lax.broadcasted_iota(jnp.int32, (group * T, T), 1)
    s_p = jnp.where((col >= row) & (n > 0), s[:, :T], MASKED)
    s_c = s[:, T:2 * T]
    s_n = jnp.where((col <= row) & (n < n_blocks - 1), s[:, 2 * T:3 * T], MASKED)
    s_x = s[:, 3 * T:]
    sink = jnp.concatenate([jnp.full((T, 1), sink_ref[kv * group + g], F32) for g in range(group)], axis=0)
    m = jnp.maximum(jnp.maximum(jnp.max(s_p, axis=1, keepdims=True), jnp.max(s_c, axis=1, keepdims=True)),
                    jnp.maximum(jnp.max(s_n, axis=1, keepdims=True), jnp.max(s_x, axis=1, keepdims=True)))
    m = jnp.maximum(m, sink)
    e_p, e_c, e_n, e_x = jnp.exp(s_p - m), jnp.exp(s_c - m), jnp.exp(s_n - m), jnp.exp(s_x - m)
    total = (jnp.sum(e_p, axis=1, keepdims=True) + jnp.sum(e_c, axis=1, keepdims=True)
             + jnp.sum(e_n, axis=1, keepdims=True) + jnp.sum(e_x, axis=1, keepdims=True) + jnp.exp(sink - m))
    inv = 1.0 / total
    p = jnp.concatenate([(e_p * inv).astype(BF16), (e_c * inv).astype(BF16), (e_n * inv).astype(BF16),
                         (e_x * inv).astype(BF16)], axis=1)
    o = jnp.dot(p, v, preferred_element_type=F32)
    for g in range(group):
        o_ref[0, :, g * LANES:(g + 1) * LANES] = o[g * T:(g + 1) * T].astype(o_ref.dtype)


def _window_attention(p, sink, *, n_ctx_rows, heads, kv_heads):
    B, S, _ = p.shape
    T = S - n_ctx_rows
    assert WINDOW == Q_BLOCK and T % Q_BLOCK == 0 and n_ctx_rows % Q_BLOCK == 0
    nb = T // Q_BLOCK
    cb = n_ctx_rows // Q_BLOCK
    group = heads // kv_heads
    kcol, vcol = heads, heads + kv_heads

    def blk(colbase, shift):
        def index(b, kv, n):
            return (b, jnp.clip(n + shift, 0, nb - 1) + cb, colbase + kv)
        return pl.BlockSpec((1, Q_BLOCK, LANES), index)

    def ctx(colbase):
        return pl.BlockSpec((1, n_ctx_rows, LANES), lambda b, kv, n: (b, 0, colbase + kv))

    return pl.pallas_call(
        functools.partial(_attention_kernel, group=group, n_blocks=nb, scale=LANES ** -0.5),
        out_shape=jax.ShapeDtypeStruct((B, T, heads * LANES), BF16),
        grid=(B, kv_heads, nb),
        in_specs=[pl.BlockSpec(memory_space=pltpu.SMEM),
                  pl.BlockSpec((1, Q_BLOCK, group * LANES), lambda b, kv, n: (b, n + cb, kv)),
                  blk(kcol, -1), blk(kcol, 0), blk(kcol, 1), ctx(kcol),
                  blk(vcol, -1), blk(vcol, 0), blk(vcol, 1), ctx(vcol)],
        out_specs=pl.BlockSpec((1, Q_BLOCK, group * LANES), lambda b, kv, n: (b, n, kv)),
        compiler_params=_params(("parallel", "parallel", "arbitrary"), 0),
        name="window_attention",
    )(sink, p, p, p, p, p, p, p, p, p)


def _rope_tables(n_ctx_rows, seq):
    quarter = LANES // 4
    freqs = ROPE_THETA ** (-jnp.arange(quarter, dtype=F32) / quarter)
    t = jnp.arange(seq)
    ang_r = (t // GRID_W).astype(F32)[:, None] * freqs[None, :]
    ang_c = (t % GRID_W).astype(F32)[:, None] * freqs[None, :]
    cos = jnp.concatenate([jnp.cos(ang_r), jnp.cos(ang_r), jnp.cos(ang_c), jnp.cos(ang_c)], axis=-1)
    sin = jnp.concatenate([-jnp.sin(ang_r), jnp.sin(ang_r), -jnp.sin(ang_c), jnp.sin(ang_c)], axis=-1)
    cos = jnp.concatenate([jnp.ones((n_ctx_rows, LANES), F32), cos], axis=0)
    sin = jnp.concatenate([jnp.zeros((n_ctx_rows, LANES), F32), sin], axis=0)
    return cos, sin


def _router_kernel(w_ref, f_ref, idx_ref, wt_ref):
    logits = lax.dot_general(w_ref[...].astype(BF16), f_ref[...].astype(BF16), (((1,), (1,)), ((), ())),
                             preferred_element_type=F32)
    n_exp = logits.shape[0]
    eid = lax.broadcasted_iota(jnp.int32, logits.shape, 0)
    v1 = jnp.max(logits, axis=0, keepdims=True)
    i1 = jnp.min(jnp.where(logits == v1, eid, n_exp), axis=0, keepdims=True)
    rest = jnp.where(eid == i1, -jnp.inf, logits)
    v2 = jnp.max(rest, axis=0, keepdims=True)
    i2 = jnp.min(jnp.where(rest == v2, eid, n_exp), axis=0, keepdims=True)
    e2 = jnp.exp(v2 - v1)
    idx_ref[0:1, :] = i1
    idx_ref[1:2, :] = i2
    wt_ref[0:1, :] = 1.0 / (1.0 + e2)
    wt_ref[1:2, :] = e2 / (1.0 + e2)


def _router(f, w_router_t):
    T, D = f.shape
    E = w_router_t.shape[0]
    tt = _divisor(T, 1024, LANES)
    return pl.pallas_call(
        _router_kernel,
        out_shape=(jax.ShapeDtypeStruct((TOP_K, T), jnp.int32), jax.ShapeDtypeStruct((TOP_K, T), F32)),
        grid=(T // tt,),
        in_specs=[pl.BlockSpec((E, D), lambda i: (0, 0)), pl.BlockSpec((tt, D), lambda i: (i, 0))],
        out_specs=(pl.BlockSpec((TOP_K, tt), lambda i: (0, i)), pl.BlockSpec((TOP_K, tt), lambda i: (0, i))),
        compiler_params=_params(("arbitrary",), 48 << 20),
        name="moe_router",
    )(w_router_t, f)


GATHER_ROWS = 256
GATHER_UNROLL = 8


def _gather_kernel(idx_ref, src_ref, o_ref, *scratch):
    sem = scratch[-1]
    dst = scratch[0] if len(scratch) == 2 else o_ref
    base = pl.program_id(0) * GATHER_ROWS

    def issue(r, carry):
        pltpu.make_async_copy(src_ref.at[pl.ds(idx_ref[base + r], 1), :], dst.at[pl.ds(r, 1), :], sem).start()
        return carry

    lax.fori_loop(0, GATHER_ROWS, issue, 0, unroll=GATHER_UNROLL)
    pltpu.make_async_copy(src_ref.at[pl.ds(0, GATHER_ROWS), :], dst, sem).wait()
    if len(scratch) == 2:
        o_ref[...] = dst[...].astype(o_ref.dtype)


def _gather_rows(src, idx, out_dtype):
    N, D = src.shape
    R = idx.shape[0]
    assert R % GATHER_ROWS == 0 and D % LANES == 0
    staged = jnp.dtype(out_dtype) != src.dtype
    scratch = ([pltpu.VMEM((GATHER_ROWS, D), src.dtype)] if staged else []) + [pltpu.SemaphoreType.DMA(())]
    return pl.pallas_call(
        _gather_kernel,
        out_shape=jax.ShapeDtypeStruct((R, D), out_dtype),
        grid_spec=pltpu.PrefetchScalarGridSpec(
            num_scalar_prefetch=1,
            grid=(R // GATHER_ROWS,),
            in_specs=[pl.BlockSpec(memory_space=pl.ANY)],
            out_specs=pl.BlockSpec((GATHER_ROWS, D), lambda i, idx_ref: (i, 0)),
            scratch_shapes=scratch,
        ),
        compiler_params=_params(("arbitrary",), 0),
        name="gather_rows",
    )(idx, src)


def _expert_up_kernel(te_ref, tv_ref, x_ref, wg_ref, wu_ref, o_ref, wgb_ref, wub_ref):
    i = pl.program_id(1)
    fresh = jnp.logical_or(i == 0, te_ref[i] != te_ref[jnp.maximum(i - 1, 0)])

    @pl.when(fresh)
    def _():
        wgb_ref[...] = wg_ref[...].astype(BF16)
        wub_ref[...] = wu_ref[...].astype(BF16)

    @pl.when(tv_ref[i] != 0)
    def _():
        x = x_ref[...]
        g = jnp.dot(x, wgb_ref[...], preferred_element_type=F32)
        u = jnp.dot(x, wub_ref[...], preferred_element_type=F32)
        o_ref[...] = (g * jax.nn.sigmoid(g) * u).astype(o_ref.dtype)

    @pl.when(tv_ref[i] == 0)
    def _():
        o_ref[...] = jnp.zeros_like(o_ref)


def _expert_up(xs, w_gu, layer, tile_expert, tile_valid, *, tn=256):
    P, D = xs.shape
    F2 = w_gu.shape[3]
    F = F2 // 2
    tm = MOE_ROWS
    tn = _divisor(F, tn, LANES)
    nf = F // tn
    vmem = (2 * _nbytes((tm, D), BF16) + 4 * _nbytes((D, tn), F32) + 2 * _nbytes((D, tn), BF16)
            + 2 * _nbytes((tm, tn), BF16) + 3 * _nbytes((tm, tn), F32) + (4 << 20))
    return pl.pallas_call(
        _expert_up_kernel,
        out_shape=jax.ShapeDtypeStruct((P, F), BF16),
        grid_spec=pltpu.PrefetchScalarGridSpec(
            num_scalar_prefetch=2,
            grid=(nf, P // tm),
            in_specs=[pl.BlockSpec((tm, D), lambda j, i, te, tv: (i, 0)),
                      pl.BlockSpec((None, None, D, tn), lambda j, i, te, tv: (layer, te[i], 0, j)),
                      pl.BlockSpec((None, None, D, tn), lambda j, i, te, tv: (layer, te[i], 0, j + nf))],
            out_specs=pl.BlockSpec((tm, tn), lambda j, i, te, tv: (i, j)),
            scratch_shapes=[pltpu.VMEM((D, tn), BF16), pltpu.VMEM((D, tn), BF16)],
        ),
        compiler_params=_params(("arbitrary", "arbitrary"), vmem),
        name="expert_up",
    )(tile_expert, tile_valid, xs, w_gu, w_gu)


def _expert_down_kernel(te_ref, tv_ref, a_ref, w_ref, gw_ref, o_ref, wb_ref):
    i = pl.program_id(1)
    fresh = jnp.logical_or(i == 0, te_ref[i] != te_ref[jnp.maximum(i - 1, 0)])

    @pl.when(fresh)
    def _():
        wb_ref[...] = w_ref[...].astype(BF16)

    @pl.when(tv_ref[i] != 0)
    def _():
        o_ref[...] = gw_ref[...] * jnp.dot(a_ref[...], wb_ref[...], preferred_element_type=F32)

    @pl.when(tv_ref[i] == 0)
    def _():
        o_ref[...] = jnp.zeros_like(o_ref)


def _expert_down(act, w_down, layer, row_gate, tile_expert, tile_valid, *, tn=512):
    P, F = act.shape
    D = w_down.shape[3]
    tm = MOE_ROWS
    tn = _divisor(D, tn, LANES)
    vmem = (2 * _nbytes((tm, F), BF16) + 2 * _nbytes((F, tn), F32) + _nbytes((F, tn), BF16)
            + 3 * _nbytes((tm, tn), F32) + (4 << 20))
    return pl.pallas_call(
        _expert_down_kernel,
        out_shape=jax.ShapeDtypeStruct((P, D), F32),
        grid_spec=pltpu.PrefetchScalarGridSpec(
            num_scalar_prefetch=2,
            grid=(D // tn, P // tm),
            in_specs=[pl.BlockSpec((tm, F), lambda j, i, te, tv: (i, 0)),
                      pl.BlockSpec((None, None, F, tn), lambda j, i, te, tv: (layer, te[i], 0, j)),
                      pl.BlockSpec((tm, 1), lambda j, i, te, tv: (i, 0))],
            out_specs=pl.BlockSpec((tm, tn), lambda j, i, te, tv: (i, j)),
            scratch_shapes=[pltpu.VMEM((F, tn), BF16)],
        ),
        compiler_params=_params(("arbitrary", "arbitrary"), vmem),
        name="expert_down",
    )(tile_expert, tile_valid, act, w_down, row_gate.reshape(P, 1))


def _routing_tables(idx, wts, n_experts):
    K, T = idx.shape
    tm = MOE_ROWS
    n_tiles = (K * T) // tm + n_experts
    P = n_tiles * tm
    e = idx.reshape(-1)
    onehot = (e[:, None] == jnp.arange(n_experts)[None, :]).astype(jnp.int32)
    csum = jnp.cumsum(onehot, axis=0)
    rank = jnp.sum(csum * onehot, axis=1) - 1
    counts = csum[-1]
    tiles_per = (counts + tm - 1) // tm
    tile_end = jnp.cumsum(tiles_per)
    start = (tile_end - tiles_per) * tm
    slot = jnp.sum(onehot * start[None, :], axis=1) + rank
    token = jnp.tile(jnp.arange(T, dtype=jnp.int32), K)
    slot_token = jnp.zeros((P,), jnp.int32).at[slot].set(token)
    slot_gate = jnp.zeros((P,), F32).at[slot].set(wts.reshape(-1))
    tiles = jnp.arange(n_tiles, dtype=jnp.int32)
    tile_valid = (tiles < tile_end[-1]).astype(jnp.int32)
    owner = jnp.sum((tile_end[None, :] <= tiles[:, None]).astype(jnp.int32), axis=1)
    last_owner = jnp.sum((tile_end <= tile_end[-1] - 1).astype(jnp.int32))
    tile_expert = jnp.where(tile_valid != 0, owner, last_owner).astype(jnp.int32)
    return slot_token, slot_gate, slot.reshape(K, T).astype(jnp.int32), tile_expert, tile_valid


def _final_ln_kernel(h_ref, y_ref, m_ref, g_ref, b_ref, o_ref, *, gate):
    m = m_ref[0]
    y = y_ref[0, 0] + y_ref[1, 0]
    o_ref[0] = _layer_norm_rows(ALPHA * h_ref[0] + m[gate:gate + 1] * y, g_ref[...], b_ref[...])


def _final_ln(h, y2, mods, ln_g, ln_b, *, gate):
    B, T, D = h.shape
    tr = _divisor(T, ROW_TILE, 16)
    row = pl.BlockSpec((1, tr, D), lambda b, r: (b, r, 0))
    vec = pl.BlockSpec((1, D), lambda b, r: (0, 0))
    return pl.pallas_call(
        functools.partial(_final_ln_kernel, gate=gate),
        out_shape=jax.ShapeDtypeStruct((B, T, D), F32),
        grid=(B, T // tr),
        in_specs=[row, pl.BlockSpec((2, 1, tr, D), lambda b, r: (0, b, r, 0)),
                  pl.BlockSpec((1, 6, D), lambda b, r: (b, 0, 0)), vec, vec],
        out_specs=row,
        compiler_params=_params(("parallel", "arbitrary"), 48 << 20),
        name="final_ln",
    )(h, y2, mods, ln_g.reshape(1, D), ln_b.reshape(1, D))


SHIFT_MIX, SCALE_MIX, GATE_MIX, SHIFT_FFN, SCALE_FFN, GATE_FFN = range(6)


def kernel(x, c, ctx, c_ctx, ada_w, ada_b, ln_g, ln_b, ml_w_in, ml_gate_b, ml_norm_g, ml_w_out, att_w_in, att_sink,
           att_w_out, ffn_w_gu, ffn_w_down, moe_router, moe_w_gu, moe_w_down):
    B, T, D = x.shape
    C = ctx.shape[1]
    S = C + T
    assert ada_w.shape[0] == DEPTH == 2
    qk, vw = ml_w_out.shape[1] // 2, ml_w_out.shape[1]
    n_experts = moe_router.shape[-1]

    cond = jnp.concatenate([c, c_ctx[None, :]], axis=0)
    mods = [_adaln(cond, ada_w, ada_b, i) for i in range(DEPTH)]

    h0 = jnp.concatenate([ctx, x], axis=1)

    a = _modulate(h0, mods[0], n_ctx_rows=C, shift=SHIFT_MIX, scale=SCALE_MIX)
    n_main = 2 * qk + 2 * vw
    p = _matmul(a.reshape(B * S, D), ml_w_in, 0, n_out=n_main, out_dtype=BF16, name="mlstm_in").reshape(B, S, n_main)
    gates = _mlstm_gates(a, ml_w_in[0, :, n_main:].T, ml_gate_b[0])
    hfb = _mlstm_scan(p, gates, n_ctx_rows=C, heads=ML_HEADS, qk=qk, vw=vw)
    hn = _mlstm_norm(hfb, p, ml_norm_g[0], heads=ML_HEADS, o_col_block=(2 * qk + vw) // vw)
    y = _matmul(hn.reshape(B * S, vw), ml_w_out, 0, name="mlstm_out").reshape(B, S, D)
    h1, f = _resid_ln(h0, y, mods[0], mods[0], ln_g[0, 0], ln_b[0, 0], n_ctx_rows=C, h_row_offset=0,
                      gate=GATE_MIX, shift=SHIFT_FFN, scale=SCALE_FFN)
    act = _swiglu_up(f.reshape(B * S, D), ffn_w_gu, 0)
    y = _matmul(act, _cast_bf16(ffn_w_down), 0, tn=256, a_buffers=1, name="ffn_down").reshape(B, S, D)
    h2, a = _resid_ln(h1, y, mods[0], mods[1], ln_g[0, 1], ln_b[0, 1], n_ctx_rows=C, h_row_offset=0,
                      gate=GATE_FFN, shift=SHIFT_MIX, scale=SCALE_MIX)

    q_cols = ATT_HEADS * LANES
    cos, sin = _rope_tables(C, T)
    p = _qkv_rope(a.reshape(B * S, D), att_w_in, 0, cos, sin, rope_cols=q_cols + ATT_KV * LANES,
                  rows_per_batch=S).reshape(B, S, -1)
    o = _window_attention(p, att_sink[0], n_ctx_rows=C, heads=ATT_HEADS, kv_heads=ATT_KV)
    y = _matmul(o.reshape(B * T, q_cols), att_w_out, 0, name="att_out").reshape(B, T, D)
    h3, f = _resid_ln(h2, y, mods[1], mods[1], ln_g[1, 0], ln_b[1, 0], n_ctx_rows=0, h_row_offset=C,
                      gate=GATE_MIX, shift=SHIFT_FFN, scale=SCALE_FFN, f_dtype=F32)

    f2 = f.reshape(B * T, D)
    idx, wts = _router(f2, moe_router[0].T)
    slot_token, slot_gate, pair_slot, tile_expert, tile_valid = _routing_tables(idx, wts, n_experts)
    xs = _gather_rows(f2, slot_token, BF16)
    act = _expert_up(xs, moe_w_gu, 0, tile_expert, tile_valid)
    ys = _expert_down(act, moe_w_down, 0, slot_gate, tile_expert, tile_valid)
    y2 = _gather_rows(ys, pair_slot.reshape(-1), F32).reshape(TOP_K, B, T, D)
    return _final_ln(h3, y2, mods[1], ln_g[1, 1], ln_b[1, 1], gate=GATE_FFN)
```

```python
import functools
import math

import jax
import jax.numpy as jnp
from jax import lax
from jax.experimental import pallas as pl
from jax.experimental.pallas import tpu as pltpu

F32 = jnp.float32
BF16 = jnp.bfloat16

GRID_W = 64
ML_HEADS = 8
ATT_HEADS = 32
ATT_KV = 8
WINDOW = 128
Q_BLOCK = 128
ROPE_THETA = 10000.0
TOP_K = 2
LN_EPS = 1e-5
RMS_EPS = 1e-6
DEPTH = 2
ALPHA = (2 * DEPTH) ** 0.25

V7X_VMEM_BYTES = 64 * 1024 * 1024
VMEM_CAP = V7X_VMEM_BYTES - 8 * 1024 * 1024
LANES = 128
MASKED = -1e30

ROW_TILE = 256
MM_ROWS = 1088
ML_CHUNK = 256
MOE_ROWS = 512


def _divisor(n, target, mult):
    best = None
    for d in range(mult, min(n, target) + 1, mult):
        if n % d == 0:
            best = d
    return best if best is not None else n


def _params(sem, vmem_bytes):
    limit = int(min(VMEM_CAP, max(32 * 1024 * 1024, vmem_bytes)))
    return pltpu.CompilerParams(dimension_semantics=sem, vmem_limit_bytes=limit)


def _nbytes(shape, dtype):
    return math.prod(shape) * jnp.dtype(dtype).itemsize


def _mm_kernel(a_ref, w_ref, o_ref):
    o_ref[...] = jnp.dot(a_ref[...], w_ref[...].astype(BF16), preferred_element_type=F32).astype(o_ref.dtype)


def _matmul(a, w, layer, *, n_out=None, out_dtype=F32, tn=512, a_buffers=2, name="matmul"):
    M, K = a.shape
    n_out = w.shape[2] if n_out is None else n_out
    tm = _divisor(M, MM_ROWS, 16)
    tn = _divisor(n_out, tn, LANES)
    vmem = (a_buffers * _nbytes((tm, K), BF16) + 2 * _nbytes((K, tn), w.dtype) + _nbytes((K, tn), BF16)
            + 2 * _nbytes((tm, tn), out_dtype) + _nbytes((tm, tn), F32) + (4 << 20))
    a_mode = {} if a_buffers == 2 else {"pipeline_mode": pl.Buffered(a_buffers)}
    return pl.pallas_call(
        _mm_kernel,
        out_shape=jax.ShapeDtypeStruct((M, n_out), out_dtype),
        grid=(M // tm, n_out // tn),
        in_specs=[pl.BlockSpec((tm, K), lambda i, j: (i, 0), **a_mode),
                  pl.BlockSpec((None, K, tn), lambda i, j: (layer, 0, j))],
        out_specs=pl.BlockSpec((tm, tn), lambda i, j: (i, j)),
        compiler_params=_params(("parallel", "arbitrary"), vmem),
        name=name,
    )(a, w)


def _cast_kernel(x_ref, o_ref):
    o_ref[...] = x_ref[...].astype(o_ref.dtype)


def _cast_bf16(w):
    L, K, N = w.shape
    tk = _divisor(K, 512, 16)
    return pl.pallas_call(
        _cast_kernel,
        out_shape=jax.ShapeDtypeStruct((L, K, N), BF16),
        grid=(L, K // tk),
        in_specs=[pl.BlockSpec((None, tk, N), lambda l, k: (l, k, 0))],
        out_specs=pl.BlockSpec((None, tk, N), lambda l, k: (l, k, 0)),
        compiler_params=_params(("parallel", "arbitrary"), 48 << 20),
        name="cast_bf16",
    )(w)


def _swiglu_up_kernel(a_ref, wg_ref, wu_ref, o_ref):
    a = a_ref[...]
    g = jnp.dot(a, wg_ref[...].astype(BF16), preferred_element_type=F32)
    u = jnp.dot(a, wu_ref[...].astype(BF16), preferred_element_type=F32)
    o_ref[...] = (g * jax.nn.sigmoid(g) * u).astype(o_ref.dtype)


def _swiglu_up(a, w_gu, layer, *, tn=256, name="swiglu_up"):
    M, K = a.shape
    F = w_gu.shape[2] // 2
    tm = _divisor(M, MM_ROWS, 16)
    tn = _divisor(F, tn, LANES)
    nf = F // tn
    vmem = (2 * _nbytes((tm, K), BF16) + 4 * _nbytes((K, tn), F32) + 2 * _nbytes((K, tn), BF16)
            + 2 * _nbytes((tm, tn), BF16) + 3 * _nbytes((tm, tn), F32) + (4 << 20))
    return pl.pallas_call(
        _swiglu_up_kernel,
        out_shape=jax.ShapeDtypeStruct((M, F), BF16),
        grid=(M // tm, nf),
        in_specs=[pl.BlockSpec((tm, K), lambda i, j: (i, 0)),
                  pl.BlockSpec((None, K, tn), lambda i, j: (layer, 0, j)),
                  pl.BlockSpec((None, K, tn), lambda i, j: (layer, 0, j + nf))],
        out_specs=pl.BlockSpec((tm, tn), lambda i, j: (i, j)),
        compiler_params=_params(("parallel", "arbitrary"), vmem),
        name=name,
    )(a, w_gu, w_gu)


def _adaln_kernel(c_ref, w_ref, b_ref, o_ref):
    c = c_ref[...]
    a = (c * jax.nn.sigmoid(c)).astype(BF16)
    o_ref[...] = jnp.dot(a, w_ref[...].astype(BF16), preferred_element_type=F32) + b_ref[...]


def _adaln(cond, w, b, layer, *, tn=512):
    R, D = cond.shape
    N = w.shape[2]
    tn = _divisor(N, tn, LANES)
    vmem = 2 * _nbytes((D, tn), F32) + _nbytes((D, tn), BF16) + (4 << 20)
    out = pl.pallas_call(
        _adaln_kernel,
        out_shape=jax.ShapeDtypeStruct((R, N), F32),
        grid=(N // tn,),
        in_specs=[pl.BlockSpec((R, D), lambda j: (0, 0)),
                  pl.BlockSpec((None, D, tn), lambda j: (layer, 0, j)),
                  pl.BlockSpec((None, 1, tn), lambda j: (layer, 0, j))],
        out_specs=pl.BlockSpec((R, tn), lambda j: (0, j)),
        compiler_params=_params(("arbitrary",), vmem),
        name="adaln",
    )(cond, w, b.reshape(b.shape[0], 1, N))
    return out.reshape(R, 6, D)


def _stream_specs(n_ctx_tiles, tr, D):
    ctx = pl.BlockSpec((1, tr, D), lambda b, r: (b, jnp.minimum(r, n_ctx_tiles - 1), 0))
    lat = pl.BlockSpec((1, tr, D), lambda b, r: (b, jnp.maximum(r - n_ctx_tiles, 0), 0))
    return ctx, lat


def _modulate_kernel(c_ref, x_ref, m_ref, o_ref, *, n_ctx_tiles, shift, scale):
    m = m_ref[0]

    def emit(h_ref):
        o_ref[0] = (h_ref[0] * (1.0 + m[scale:scale + 1]) + m[shift:shift + 1]).astype(o_ref.dtype)

    pl.when(pl.program_id(1) < n_ctx_tiles)(lambda: emit(c_ref))
    pl.when(pl.program_id(1) >= n_ctx_tiles)(lambda: emit(x_ref))


def _modulate(ctx, x, mods, *, shift, scale):
    B, C, D = ctx.shape
    S = C + x.shape[1]
    tr = _divisor(math.gcd(S, C), ROW_TILE, 16)
    nct = C // tr
    return pl.pallas_call(
        functools.partial(_modulate_kernel, n_ctx_tiles=nct, shift=shift, scale=scale),
        out_shape=jax.ShapeDtypeStruct((B, S, D), BF16),
        grid=(B, S // tr),
        in_specs=[*_stream_specs(nct, tr, D),
                  pl.BlockSpec((1, 6, D), lambda b, r: (jnp.where(r < nct, B, b), 0, 0))],
        out_specs=pl.BlockSpec((1, tr, D), lambda b, r: (b, r, 0)),
        compiler_params=_params(("parallel", "arbitrary"), 0),
        name="modulate",
    )(ctx, x, mods)


def _layer_norm_rows(z, g, b):
    mu = jnp.mean(z, axis=-1, keepdims=True)
    zc = z - mu
    var = jnp.mean(zc * zc, axis=-1, keepdims=True)
    return zc * lax.rsqrt(var + LN_EPS) * g + b


def _resid_ln_kernel(*refs, n_ctx_tiles, split_h, gate, shift, scale):
    h_refs, (y_ref, m_ref, mn_ref, g_ref, b_ref, hn_ref, f_ref) = refs[:-7], refs[-7:]
    m = m_ref[0]
    mn = mn_ref[0]

    def emit(h_ref):
        hn = _layer_norm_rows(ALPHA * h_ref[0] + m[gate:gate + 1] * y_ref[0], g_ref[...], b_ref[...])
        hn_ref[0] = hn
        f_ref[0] = (hn * (1.0 + mn[scale:scale + 1]) + mn[shift:shift + 1]).astype(f_ref.dtype)

    if split_h:
        pl.when(pl.program_id(1) < n_ctx_tiles)(lambda: emit(h_refs[0]))
        pl.when(pl.program_id(1) >= n_ctx_tiles)(lambda: emit(h_refs[1]))
    else:
        emit(h_refs[0])


def _resid_ln(h, y, mods, mods_next, ln_g, ln_b, *, n_ctx_rows, h_row_offset, gate, shift, scale, f_dtype=BF16):
    B, R, D = y.shape
    tr = _divisor(math.gcd(R, n_ctx_rows, h_row_offset), ROW_TILE, 16)
    nct = n_ctx_rows // tr
    off = h_row_offset // tr
    cond = lambda b, r: (jnp.where(r < nct, B, b), 0, 0)
    row = pl.BlockSpec((1, tr, D), lambda b, r: (b, r, 0))
    vec = pl.BlockSpec((1, D), lambda b, r: (0, 0))
    split_h = isinstance(h, tuple)
    if split_h:
        assert off == 0 and nct > 0
        h_arrays, h_specs = list(h), list(_stream_specs(nct, tr, D))
    else:
        h_arrays, h_specs = [h], [pl.BlockSpec((1, tr, D), lambda b, r: (b, r + off, 0))]
    return pl.pallas_call(
        functools.partial(_resid_ln_kernel, n_ctx_tiles=nct, split_h=split_h, gate=gate, shift=shift, scale=scale),
        out_shape=(jax.ShapeDtypeStruct((B, R, D), F32), jax.ShapeDtypeStruct((B, R, D), f_dtype)),
        grid=(B, R // tr),
        in_specs=[*h_specs, row, pl.BlockSpec((1, 6, D), cond), pl.BlockSpec((1, 6, D), cond), vec, vec],
        out_specs=(row, row),
        compiler_params=_params(("parallel", "arbitrary"), 48 << 20),
        name="resid_ln",
    )(*h_arrays, y, mods, mods_next, ln_g.reshape(1, D), ln_b.reshape(1, D))


def _gates_kernel(w_ref, a_ref, b_ref, o_ref):
    g = lax.dot_general(w_ref[...].astype(BF16), a_ref[0], (((1,), (1,)), ((), ())),
                        preferred_element_type=F32)
    o_ref[0] = g + b_ref[...]


def _mlstm_gates(a, w_gate_t, gate_b):
    B, S, D = a.shape
    G = w_gate_t.shape[0]
    ts = _divisor(S, 1088, LANES)
    return pl.pallas_call(
        _gates_kernel,
        out_shape=jax.ShapeDtypeStruct((B, G, S), F32),
        grid=(B, S // ts),
        in_specs=[pl.BlockSpec((G, D), lambda b, s: (0, 0)),
                  pl.BlockSpec((1, ts, D), lambda b, s: (b, s, 0)),
                  pl.BlockSpec((G, 1), lambda b, s: (0, 0))],
        out_specs=pl.BlockSpec((1, G, ts), lambda b, s: (b, 0, s)),
        compiler_params=_params(("parallel", "arbitrary"), 0),
        name="mlstm_gates",
    )(w_gate_t, a, gate_b.reshape(G, 1))


def _log_sigmoid(x):
    return jnp.minimum(x, 0.0) - jnp.log1p(jnp.exp(-jnp.abs(x)))


def _mlstm_scan_kernel(g_ref, q_ref, k_ref, v_ref, h_ref, c_ref, n_ref, m_ref, *, heads, dk, dv, chunk):
    direction = pl.program_id(1)
    step = pl.program_id(2)
    L = chunk

    @pl.when(step == 0)
    def _():
        c_ref[...] = jnp.zeros_like(c_ref)
        n_ref[...] = jnp.zeros_like(n_ref)
        m_ref[...] = jnp.zeros_like(m_ref)

    row = lax.broadcasted_iota(jnp.int32, (L, L), 0)
    col = lax.broadcasted_iota(jnp.int32, (L, L), 1)
    seen = jnp.where(direction == 0, col - row, row - col) <= 0
    eye = row == col

    def to_col(r):
        return jnp.sum(jnp.where(eye, r, 0.0), axis=1, keepdims=True)

    def to_row(c):
        return jnp.sum(jnp.where(eye, c, 0.0), axis=0, keepdims=True)

    for h in range(heads):
        ig = g_ref[0, pl.ds(direction * 2 * heads + h, 1), :]
        lf = _log_sigmoid(g_ref[0, pl.ds(direction * 2 * heads + heads + h, 1), :])
        q = q_ref[0, :, h * dk:(h + 1) * dk] * (dk ** -0.5)
        k = k_ref[0, :, h * dk:(h + 1) * dk]
        v = v_ref[0, :, h * dv:(h + 1) * dv]
        m_prev = m_ref[h]

        b_col = jnp.sum(jnp.where(seen, lf, 0.0), axis=1, keepdims=True)
        b_row = to_row(b_col)
        d = jnp.where(seen, b_col - b_row + ig, MASKED)
        inter = b_col + m_prev
        m_t = jnp.maximum(inter, jnp.max(d, axis=1, keepdims=True))
        s = lax.dot_general(q, k, (((1,), (1,)), ((), ())), preferred_element_type=F32) * jnp.exp(d - m_t)
        g_in = jnp.exp(inter - m_t)
        num = (g_in * jnp.dot(q, c_ref[h].astype(BF16), preferred_element_type=F32)
               + jnp.dot(s.astype(BF16), v, preferred_element_type=F32))
        qf = q.astype(F32)
        den = g_in * jnp.sum(qf * n_ref[h], axis=1, keepdims=True) + jnp.sum(s, axis=1, keepdims=True)
        h_ref[0, 0, :, h * dv:(h + 1) * dv] = num / jnp.maximum(jnp.abs(den), jnp.exp(-m_t))

        b_last = jnp.sum(lf, axis=1, keepdims=True)
        w_row = b_last - b_row + ig
        m_new = jnp.maximum(b_last + m_prev, jnp.max(w_row, axis=1, keepdims=True))
        decay = jnp.exp(b_last + m_prev - m_new)
        kw = k.astype(F32) * to_col(jnp.exp(w_row - m_new))
        c_ref[h] = decay * c_ref[h] + jnp.dot(kw.T.astype(BF16), v, preferred_element_type=F32)
        n_ref[h] = decay * n_ref[h] + jnp.sum(kw, axis=0, keepdims=True)
        m_ref[h] = m_new


def _mlstm_scan(p, gates, *, n_ctx_rows, heads, qk, vw):
    B, S, _ = p.shape
    L = _divisor(math.gcd(S, n_ctx_rows), ML_CHUNK, LANES)
    nc, ncc = S // L, n_ctx_rows // L
    dk, dv = qk // heads, vw // heads

    def chunk(d, j):
        back = jnp.where(j < ncc, ncc - 1 - j, nc - 1 - (j - ncc))
        return jnp.where(d == 0, j, back)

    vmem = (2 * (2 * _nbytes((L, qk), BF16) + _nbytes((L, vw), BF16) + _nbytes((L, vw), F32))
            + _nbytes((heads, dk, dv), F32) + (16 << 20))
    return pl.pallas_call(
        functools.partial(_mlstm_scan_kernel, heads=heads, dk=dk, dv=dv, chunk=L),
        out_shape=jax.ShapeDtypeStruct((2, B, S, vw), F32),
        grid=(B, 2, nc),
        in_specs=[pl.BlockSpec((1, 4 * heads, L), lambda b, d, j: (b, 0, chunk(d, j))),
                  pl.BlockSpec((1, L, qk), lambda b, d, j: (b, chunk(d, j), 0)),
                  pl.BlockSpec((1, L, qk), lambda b, d, j: (b, chunk(d, j), 1)),
                  pl.BlockSpec((1, L, vw), lambda b, d, j: (b, chunk(d, j), (2 * qk) // vw))],
        out_specs=pl.BlockSpec((1, 1, L, vw), lambda b, d, j: (d, b, chunk(d, j), 0)),
        scratch_shapes=[pltpu.VMEM((heads, dk, dv), F32), pltpu.VMEM((heads, 1, dk), F32),
                        pltpu.VMEM((heads, 1, 1), F32)],
        compiler_params=_params(("parallel", "parallel", "arbitrary"), vmem),
        name="mlstm_scan",
    )(gates, p, p, p)


def _mlstm_norm_kernel(h_ref, o_ref, g_ref, out_ref, *, heads, dv):
    hs = h_ref[0, 0] + h_ref[1, 0]
    for h in range(heads):
        x = hs[:, h * dv:(h + 1) * dv]
        x = x * lax.rsqrt(jnp.mean(x * x, axis=-1, keepdims=True) + RMS_EPS)
        x = x * g_ref[:, h * dv:(h + 1) * dv] * jax.nn.sigmoid(o_ref[0, :, h * dv:(h + 1) * dv].astype(F32))
        out_ref[0, :, h * dv:(h + 1) * dv] = x.astype(out_ref.dtype)


def _mlstm_norm(hfb, p, norm_g, *, heads, o_col_block):
    _, B, S, vw = hfb.shape
    tr = _divisor(S, ROW_TILE, 16)
    return pl.pallas_call(
        functools.partial(_mlstm_norm_kernel, heads=heads, dv=vw // heads),
        out_shape=jax.ShapeDtypeStruct((B, S, vw), BF16),
        grid=(B, S // tr),
        in_specs=[pl.BlockSpec((2, 1, tr, vw), lambda b, r: (0, b, r, 0)),
                  pl.BlockSpec((1, tr, vw), lambda b, r: (b, r, o_col_block)),
                  pl.BlockSpec((1, vw), lambda b, r: (0, 0))],
        out_specs=pl.BlockSpec((1, tr, vw), lambda b, r: (b, r, 0)),
        compiler_params=_params(("parallel", "arbitrary"), 48 << 20),
        name="mlstm_norm",
    )(hfb, p, norm_g.reshape(1, vw))


def _qkv_rope_kernel(a_ref, w_ref, cos_ref, sin_ref, o_ref, *, rope_blocks):
    j = pl.program_id(1)
    acc = jnp.dot(a_ref[...], w_ref[...].astype(BF16), preferred_element_type=F32)

    @pl.when(j < rope_blocks)
    def _():
        tn = acc.shape[1]
        lane = lax.broadcasted_iota(jnp.int32, (1, LANES), 1)
        first = (lane // (LANES // 4)) % 2 == 0
        for hd in range(tn // LANES):
            x = acc[:, hd * LANES:(hd + 1) * LANES]
            partner = jnp.where(first, pltpu.roll(x, LANES - LANES // 4, 1), pltpu.roll(x, LANES // 4, 1))
            o_ref[:, hd * LANES:(hd + 1) * LANES] = (x * cos_ref[...] + partner * sin_ref[...]).astype(o_ref.dtype)

    @pl.when(j >= rope_blocks)
    def _():
        o_ref[...] = acc.astype(o_ref.dtype)


def _qkv_rope(a, w, layer, cos, sin, *, rope_cols, rows_per_batch, tn=512):
    M, K = a.shape
    N = w.shape[2]
    tm = _divisor(rows_per_batch, MM_ROWS, 16)
    tn = _divisor(math.gcd(N, rope_cols), tn, LANES)
    ntab = rows_per_batch // tm
    vmem = (2 * _nbytes((tm, K), BF16) + 2 * _nbytes((K, tn), F32) + _nbytes((K, tn), BF16)
            + 2 * _nbytes((tm, tn), BF16) + 3 * _nbytes((tm, tn), F32) + (6 << 20))
    return pl.pallas_call(
        functools.partial(_qkv_rope_kernel, rope_blocks=rope_cols // tn),
        out_shape=jax.ShapeDtypeStruct((M, N), BF16),
        grid=(M // tm, N // tn),
        in_specs=[pl.BlockSpec((tm, K), lambda i, j: (i, 0)),
                  pl.BlockSpec((None, K, tn), lambda i, j: (layer, 0, j)),
                  pl.BlockSpec((tm, LANES), lambda i, j: (i % ntab, 0)),
                  pl.BlockSpec((tm, LANES), lambda i, j: (i % ntab, 0))],
        out_specs=pl.BlockSpec((tm, tn), lambda i, j: (i, j)),
        compiler_params=_params(("parallel", "arbitrary"), vmem),
        name="qkv_rope",
    )(a, w, cos, sin)


def _attention_kernel(sink_ref, q_ref, kp_ref, kc_ref, kn_ref, kx_ref, vp_ref, vc_ref, vn_ref, vx_ref, o_ref,
                      *, group, kv_step, n_blocks, scale):
    n = pl.program_id(2)
    T = q_ref.shape[1]
    row = lax.broadcasted_iota(jnp.int32, (group * T, T), 0) % T
    col = lax.broadcasted_iota(jnp.int32, (group * T, T), 1)
    prev_ok = (col >= row) & (n > 0)
    next_ok = (col <= row) & (n < n_blocks - 1)
    for h in range(kv_step):
        kv = pl.program_id(1) * kv_step + h
        hs = slice(h * LANES, (h + 1) * LANES)
        q = jnp.concatenate([q_ref[0, :, (h * group + g) * LANES:(h * group + g + 1) * LANES] for g in range(group)],
                            axis=0)
        k = jnp.concatenate([kp_ref[0, :, hs], kc_ref[0, :, hs], kn_ref[0, :, hs], kx_ref[0, :, hs]], axis=0)
        v = jnp.concatenate([vp_ref[0, :, hs], vc_ref[0, :, hs], vn_ref[0, :, hs], vx_ref[0, :, hs]], axis=0)
        s = lax.dot_general(q, k, (((1,), (1,)), ((), ())), preferred_element_type=F32) * scale
        s_p = jnp.where(prev_ok, s[:, :T], MASKED)
        s_c = s[:, T:2 * T]
        s_n = jnp.where(next_ok, s[:, 2 * T:3 * T], MASKED)
        s_x = s[:, 3 * T:]
        sink = jnp.concatenate([jnp.full((T, 1), sink_ref[kv * group + g], F32) for g in range(group)], axis=0)
        m = jnp.maximum(jnp.maximum(jnp.max(s_p, axis=1, keepdims=True), jnp.max(s_c, axis=1, keepdims=True)),
                        jnp.maximum(jnp.max(s_n, axis=1, keepdims=True), jnp.max(s_x, axis=1, keepdims=True)))
        m = jnp.maximum(m, sink)
        e_p, e_c, e_n, e_x = jnp.exp(s_p - m), jnp.exp(s_c - m), jnp.exp(s_n - m), jnp.exp(s_x - m)
        total = (jnp.sum(e_p, axis=1, keepdims=True) + jnp.sum(e_c, axis=1, keepdims=True)
                 + jnp.sum(e_n, axis=1, keepdims=True) + jnp.sum(e_x, axis=1, keepdims=True) + jnp.exp(sink - m))
        inv = 1.0 / total
        p = jnp.concatenate([(e_p * inv).astype(BF16), (e_c * inv).astype(BF16), (e_n * inv).astype(BF16),
                             (e_x * inv).astype(BF16)], axis=1)
        o = jnp.dot(p, v, preferred_element_type=F32)
        for g in range(group):
            o_ref[0, :, (h * group + g) * LANES:(h * group + g + 1) * LANES] = o[g * T:(g + 1) * T].astype(o_ref.dtype)


def _window_attention(p, sink, *, n_ctx_rows, heads, kv_heads):
    B, S, _ = p.shape
    T = S - n_ctx_rows
    assert WINDOW == Q_BLOCK and T % Q_BLOCK == 0 and n_ctx_rows % Q_BLOCK == 0
    nb = T // Q_BLOCK
    cb = n_ctx_rows // Q_BLOCK
    group = heads // kv_heads
    kv_step = 2 if kv_heads % 2 == 0 else 1
    kw = kv_step * LANES
    kcol, vcol = heads // kv_step, (heads + kv_heads) // kv_step

    def blk(colbase, shift):
        def index(b, kv, n):
            return (b, jnp.clip(n + shift, 0, nb - 1) + cb, colbase + kv)
        return pl.BlockSpec((1, Q_BLOCK, kw), index)

    def ctx(colbase):
        return pl.BlockSpec((1, n_ctx_rows, kw), lambda b, kv, n: (b, 0, colbase + kv))

    qw = kv_step * group * LANES
    return pl.pallas_call(
        functools.partial(_attention_kernel, group=group, kv_step=kv_step, n_blocks=nb, scale=LANES ** -0.5),
        out_shape=jax.ShapeDtypeStruct((B, T, heads * LANES), BF16),
        grid=(B, kv_heads // kv_step, nb),
        in_specs=[pl.BlockSpec(memory_space=pltpu.SMEM),
                  pl.BlockSpec((1, Q_BLOCK, qw), lambda b, kv, n: (b, n + cb, kv)),
                  blk(kcol, -1), blk(kcol, 0), blk(kcol, 1), ctx(kcol),
                  blk(vcol, -1), blk(vcol, 0), blk(vcol, 1), ctx(vcol)],
        out_specs=pl.BlockSpec((1, Q_BLOCK, qw), lambda b, kv, n: (b, n, kv)),
        compiler_params=_params(("parallel", "parallel", "arbitrary"), 0),
        name="window_attention",
    )(sink, p, p, p, p, p, p, p, p, p)


def _rope_tables(n_ctx_rows, seq):
    quarter = LANES // 4
    freqs = ROPE_THETA ** (-jnp.arange(quarter, dtype=F32) / quarter)
    t = jnp.arange(seq)
    ang_r = (t // GRID_W).astype(F32)[:, None] * freqs[None, :]
    ang_c = (t % GRID_W).astype(F32)[:, None] * freqs[None, :]
    cos = jnp.concatenate([jnp.cos(ang_r), jnp.cos(ang_r), jnp.cos(ang_c), jnp.cos(ang_c)], axis=-1)
    sin = jnp.concatenate([-jnp.sin(ang_r), jnp.sin(ang_r), -jnp.sin(ang_c), jnp.sin(ang_c)], axis=-1)
    cos = jnp.concatenate([jnp.ones((n_ctx_rows, LANES), F32), cos], axis=0)
    sin = jnp.concatenate([jnp.zeros((n_ctx_rows, LANES), F32), sin], axis=0)
    return cos, sin


def _router_kernel(w_ref, f_ref, idx_ref, wt_ref):
    logits = lax.dot_general(w_ref[...].astype(BF16), f_ref[...].astype(BF16), (((1,), (1,)), ((), ())),
                             preferred_element_type=F32)
    n_exp = logits.shape[0]
    eid = lax.broadcasted_iota(jnp.int32, logits.shape, 0)
    v1 = jnp.max(logits, axis=0, keepdims=True)
    i1 = jnp.min(jnp.where(logits == v1, eid, n_exp), axis=0, keepdims=True)
    rest = jnp.where(eid == i1, -jnp.inf, logits)
    v2 = jnp.max(rest, axis=0, keepdims=True)
    i2 = jnp.min(jnp.where(rest == v2, eid, n_exp), axis=0, keepdims=True)
    e2 = jnp.exp(v2 - v1)
    idx_ref[0:1, :] = i1
    idx_ref[1:2, :] = i2
    wt_ref[0:1, :] = 1.0 / (1.0 + e2)
    wt_ref[1:2, :] = e2 / (1.0 + e2)


def _router(f, w_router_t):
    T, D = f.shape
    E = w_router_t.shape[0]
    tt = _divisor(T, 1024, LANES)
    return pl.pallas_call(
        _router_kernel,
        out_shape=(jax.ShapeDtypeStruct((TOP_K, T), jnp.int32), jax.ShapeDtypeStruct((TOP_K, T), F32)),
        grid=(T // tt,),
        in_specs=[pl.BlockSpec((E, D), lambda i: (0, 0)), pl.BlockSpec((tt, D), lambda i: (i, 0))],
        out_specs=(pl.BlockSpec((TOP_K, tt), lambda i: (0, i)), pl.BlockSpec((TOP_K, tt), lambda i: (0, i))),
        compiler_params=_params(("arbitrary",), 48 << 20),
        name="moe_router",
    )(w_router_t, f)


GATHER_ROWS = MOE_ROWS // 2
GATHER_UNROLL = 8


def _gather_kernel(idx_ref, th_ref, src_ref, o_ref, stage, sem):
    i = pl.program_id(0)
    base = i * GATHER_ROWS
    used = (i % 2) < th_ref[i // 2]

    @pl.when(used)
    def _():
        def issue(r, carry):
            pltpu.make_async_copy(src_ref.at[pl.ds(idx_ref[base + r], 1), :], stage.at[pl.ds(r, 1), :], sem).start()
            return carry

        lax.fori_loop(0, GATHER_ROWS, issue, 0, unroll=GATHER_UNROLL)
        pltpu.make_async_copy(src_ref.at[pl.ds(0, GATHER_ROWS), :], stage, sem).wait()
        o_ref[...] = stage[...].astype(o_ref.dtype)

    @pl.when(jnp.logical_not(used))
    def _():
        o_ref[...] = jnp.zeros_like(o_ref)


def _gather_rows(src, idx, tile_halves, out_dtype):
    N, D = src.shape
    R = idx.shape[0]
    assert R % MOE_ROWS == 0 and D % LANES == 0
    return pl.pallas_call(
        _gather_kernel,
        out_shape=jax.ShapeDtypeStruct((R, D), out_dtype),
        grid_spec=pltpu.PrefetchScalarGridSpec(
            num_scalar_prefetch=2,
            grid=(R // GATHER_ROWS,),
            in_specs=[pl.BlockSpec(memory_space=pl.ANY)],
            out_specs=pl.BlockSpec((GATHER_ROWS, D), lambda i, idx_ref, th_ref: (i, 0)),
            scratch_shapes=[pltpu.VMEM((GATHER_ROWS, D), src.dtype), pltpu.SemaphoreType.DMA(())],
        ),
        compiler_params=_params(("arbitrary",), 0),
        name="gather_rows",
    )(idx, tile_halves, src)


def _for_used_halves(halves, o_ref, emit):
    half = o_ref.shape[0] // 2

    @pl.when(halves == 2)
    def _():
        emit(slice(None))

    @pl.when(halves == 1)
    def _():
        emit(slice(0, half))
        o_ref[half:, :] = jnp.zeros((half, o_ref.shape[1]), o_ref.dtype)

    @pl.when(halves == 0)
    def _():
        o_ref[...] = jnp.zeros_like(o_ref)


def _expert_up_kernel(te_ref, th_ref, ts_ref, x_ref, wg_ref, wu_ref, o_ref, wgb_ref, wub_ref):
    i = pl.program_id(1)
    fresh = jnp.logical_or(i == 0, te_ref[i] != te_ref[jnp.maximum(i - 1, 0)])

    @pl.when(fresh)
    def _():
        wgb_ref[...] = wg_ref[...].astype(BF16)
        wub_ref[...] = wu_ref[...].astype(BF16)

    def emit(rows):
        x = x_ref[rows, :]
        g = jnp.dot(x, wgb_ref[...], preferred_element_type=F32)
        u = jnp.dot(x, wub_ref[...], preferred_element_type=F32)
        o_ref[rows, :] = (g * jax.nn.sigmoid(g) * u).astype(o_ref.dtype)

    _for_used_halves(th_ref[i], o_ref, emit)


def _expert_up(xs, w_gu, layer, tiles, *, tn=256):
    P, D = xs.shape
    F2 = w_gu.shape[3]
    F = F2 // 2
    tm = MOE_ROWS
    tn = _divisor(F, tn, LANES)
    nf = F // tn
    vmem = (2 * _nbytes((tm, D), BF16) + 4 * _nbytes((D, tn), F32) + 2 * _nbytes((D, tn), BF16)
            + 2 * _nbytes((tm, tn), BF16) + 3 * _nbytes((tm, tn), F32) + (4 << 20))
    return pl.pallas_call(
        _expert_up_kernel,
        out_shape=jax.ShapeDtypeStruct((P, F), BF16),
        grid_spec=pltpu.PrefetchScalarGridSpec(
            num_scalar_prefetch=3,
            grid=(nf, P // tm),
            in_specs=[pl.BlockSpec((tm, D), lambda j, i, te, th, ts: (ts[i], 0)),
                      pl.BlockSpec((None, None, D, tn), lambda j, i, te, th, ts: (layer, te[i], 0, j)),
                      pl.BlockSpec((None, None, D, tn), lambda j, i, te, th, ts: (layer, te[i], 0, j + nf))],
            out_specs=pl.BlockSpec((tm, tn), lambda j, i, te, th, ts: (i, j)),
            scratch_shapes=[pltpu.VMEM((D, tn), BF16), pltpu.VMEM((D, tn), BF16)],
        ),
        compiler_params=_params(("arbitrary", "arbitrary"), vmem),
        name="expert_up",
    )(*tiles, xs, w_gu, w_gu)


def _expert_down_kernel(te_ref, th_ref, ts_ref, a_ref, w_ref, o_ref, wb_ref):
    i = pl.program_id(1)
    fresh = jnp.logical_or(i == 0, te_ref[i] != te_ref[jnp.maximum(i - 1, 0)])

    @pl.when(fresh)
    def _():
        wb_ref[...] = w_ref[...].astype(BF16)

    def emit(rows):
        o_ref[rows, :] = jnp.dot(a_ref[rows, :], wb_ref[...], preferred_element_type=F32)

    _for_used_halves(th_ref[i], o_ref, emit)


def _expert_down(act, w_down, layer, tiles, *, tn=512):
    P, F = act.shape
    D = w_down.shape[3]
    tm = MOE_ROWS
    tn = _divisor(D, tn, LANES)
    vmem = (2 * _nbytes((tm, F), BF16) + 2 * _nbytes((F, tn), F32) + _nbytes((F, tn), BF16)
            + 3 * _nbytes((tm, tn), F32) + (4 << 20))
    return pl.pallas_call(
        _expert_down_kernel,
        out_shape=jax.ShapeDtypeStruct((P, D), F32),
        grid_spec=pltpu.PrefetchScalarGridSpec(
            num_scalar_prefetch=3,
            grid=(D // tn, P // tm),
            in_specs=[pl.BlockSpec((tm, F), lambda j, i, te, th, ts: (ts[i], 0)),
                      pl.BlockSpec((None, None, F, tn), lambda j, i, te, th, ts: (layer, te[i], 0, j))],
            out_specs=pl.BlockSpec((tm, tn), lambda j, i, te, th, ts: (i, j)),
            scratch_shapes=[pltpu.VMEM((F, tn), BF16)],
        ),
        compiler_params=_params(("arbitrary", "arbitrary"), vmem),
        name="expert_down",
    )(*tiles, act, w_down)


def _routing_tables(idx, n_experts):
    K, T = idx.shape
    tm = MOE_ROWS
    n_tiles = (K * T) // tm + n_experts
    P = n_tiles * tm
    e = idx.reshape(-1)
    onehot = (e[:, None] == jnp.arange(n_experts)[None, :]).astype(jnp.int32)
    csum = jnp.cumsum(onehot, axis=0)
    rank = jnp.sum(csum * onehot, axis=1) - 1
    counts = csum[-1]
    tiles_per = (counts + tm - 1) // tm
    tile_end = jnp.cumsum(tiles_per)
    start = (tile_end - tiles_per) * tm
    slot = jnp.sum(onehot * start[None, :], axis=1) + rank
    token = jnp.tile(jnp.arange(T, dtype=jnp.int32), K)
    slot_token = jnp.zeros((P,), jnp.int32).at[slot].set(token)
    tiles = jnp.arange(n_tiles, dtype=jnp.int32)
    owner = jnp.sum((tile_end[None, :] <= tiles[:, None]).astype(jnp.int32), axis=1)
    owns = (owner[:, None] == jnp.arange(n_experts)[None, :]).astype(jnp.int32)
    rows_used = jnp.clip(jnp.sum(owns * (start + counts)[None, :], axis=1) - tiles * tm, 0, tm)
    half = tm // 2
    tile_halves = ((rows_used + half - 1) // half).astype(jnp.int32)
    last_owner = jnp.sum((tile_end <= tile_end[-1] - 1).astype(jnp.int32))
    tile_expert = jnp.where(rows_used > 0, owner, last_owner).astype(jnp.int32)
    tile_src = jnp.minimum(tiles, tile_end[-1] - 1).astype(jnp.int32)
    return slot_token, slot.reshape(K, T).astype(jnp.int32), (tile_expert, tile_halves, tile_src)


def _combine_ln_kernel(slot_ref, ys_ref, h_ref, w_ref, m_ref, g_ref, b_ref, o_ref, ybuf, sems, *, gate, n_tokens):
    s = pl.program_id(0)
    tr = o_ref.shape[0]

    def issue(step, buf):
        def body(r, carry):
            for k in range(TOP_K):
                row = slot_ref[k * n_tokens + step * tr + r]
                pltpu.make_async_copy(ys_ref.at[pl.ds(row, 1), :], ybuf.at[buf, k, pl.ds(r, 1), :], sems.at[buf]).start()
            return carry
        lax.fori_loop(0, tr, body, 0, unroll=GATHER_UNROLL // TOP_K)

    @pl.when(s == 0)
    def _():
        issue(0, 0)

    @pl.when(s + 1 < pl.num_programs(0))
    def _():
        issue(s + 1, (s + 1) % 2)

    buf = s % 2
    for k in range(TOP_K):
        pltpu.make_async_copy(ys_ref.at[pl.ds(0, tr), :], ybuf.at[buf, k], sems.at[buf]).wait()
    y = w_ref[:, 0:1] * ybuf[buf, 0]
    for k in range(1, TOP_K):
        y = y + w_ref[:, k:k + 1] * ybuf[buf, k]
    m = m_ref[0]
    o_ref[...] = _layer_norm_rows(ALPHA * h_ref[...] + m[gate:gate + 1] * y, g_ref[...], b_ref[...])


def _combine_ln(h, ys, pair_slot, wts, mods, ln_g, ln_b, *, gate):
    B, T, D = h.shape
    n_tokens = B * T
    tr = _divisor(T, ROW_TILE, 16)
    row = pl.BlockSpec((tr, D), lambda i, slots: (i, 0))
    vec = pl.BlockSpec((1, D), lambda i, slots: (0, 0))
    out = pl.pallas_call(
        functools.partial(_combine_ln_kernel, gate=gate, n_tokens=n_tokens),
        out_shape=jax.ShapeDtypeStruct((n_tokens, D), F32),
        grid_spec=pltpu.PrefetchScalarGridSpec(
            num_scalar_prefetch=1,
            grid=(n_tokens // tr,),
            in_specs=[pl.BlockSpec(memory_space=pl.ANY), row,
                      pl.BlockSpec((tr, TOP_K), lambda i, slots: (i, 0)),
                      pl.BlockSpec((1, 6, D), lambda i, slots: (i // (T // tr), 0, 0)), vec, vec],
            out_specs=row,
            scratch_shapes=[pltpu.VMEM((2, TOP_K, tr, D), F32), pltpu.SemaphoreType.DMA((2,))],
        ),
        compiler_params=_params(("arbitrary",), 2 * TOP_K * _nbytes((tr, D), F32) + 5 * _nbytes((tr, D), F32) + (4 << 20)),
        name="combine_ln",
    )(pair_slot.reshape(-1), ys, h.reshape(n_tokens, D), wts.T, mods, ln_g.reshape(1, D), ln_b.reshape(1, D))
    return out.reshape(B, T, D)


SHIFT_MIX, SCALE_MIX, GATE_MIX, SHIFT_FFN, SCALE_FFN, GATE_FFN = range(6)


def kernel(x, c, ctx, c_ctx, ada_w, ada_b, ln_g, ln_b, ml_w_in, ml_gate_b, ml_norm_g, ml_w_out, att_w_in, att_sink,
           att_w_out, ffn_w_gu, ffn_w_down, moe_router, moe_w_gu, moe_w_down):
    B, T, D = x.shape
    C = ctx.shape[1]
    S = C + T
    assert ada_w.shape[0] == DEPTH == 2
    qk, vw = ml_w_out.shape[1] // 2, ml_w_out.shape[1]
    n_experts = moe_router.shape[-1]

    cond = jnp.concatenate([c, c_ctx[None, :]], axis=0)
    mods = [_adaln(cond, ada_w, ada_b, i) for i in range(DEPTH)]

    a = _modulate(ctx, x, mods[0], shift=SHIFT_MIX, scale=SCALE_MIX)
    n_main = 2 * qk + 2 * vw
    p = _matmul(a.reshape(B * S, D), ml_w_in, 0, n_out=n_main, out_dtype=BF16, name="mlstm_in").reshape(B, S, n_main)
    gates = _mlstm_gates(a, ml_w_in[0, :, n_main:].T, ml_gate_b[0])
    hfb = _mlstm_scan(p, gates, n_ctx_rows=C, heads=ML_HEADS, qk=qk, vw=vw)
    hn = _mlstm_norm(hfb, p, ml_norm_g[0], heads=ML_HEADS, o_col_block=(2 * qk + vw) // vw)
    y = _matmul(hn.reshape(B * S, vw), ml_w_out, 0, name="mlstm_out").reshape(B, S, D)
    h1, f = _resid_ln((ctx, x), y, mods[0], mods[0], ln_g[0, 0], ln_b[0, 0], n_ctx_rows=C, h_row_offset=0,
                      gate=GATE_MIX, shift=SHIFT_FFN, scale=SCALE_FFN)
    act = _swiglu_up(f.reshape(B * S, D), ffn_w_gu, 0)
    y = _matmul(act, _cast_bf16(ffn_w_down), 0, tn=256, a_buffers=1, name="ffn_down").reshape(B, S, D)
    h2, a = _resid_ln(h1, y, mods[0], mods[1], ln_g[0, 1], ln_b[0, 1], n_ctx_rows=C, h_row_offset=0,
                      gate=GATE_FFN, shift=SHIFT_MIX, scale=SCALE_MIX)

    q_cols = ATT_HEADS * LANES
    cos, sin = _rope_tables(C, T)
    p = _qkv_rope(a.reshape(B * S, D), att_w_in, 0, cos, sin, rope_cols=q_cols + ATT_KV * LANES,
                  rows_per_batch=S).reshape(B, S, -1)
    o = _window_attention(p, att_sink[0], n_ctx_rows=C, heads=ATT_HEADS, kv_heads=ATT_KV)
    y = _matmul(o.reshape(B * T, q_cols), att_w_out, 0, name="att_out").reshape(B, T, D)
    h3, f = _resid_ln(h2, y, mods[1], mods[1], ln_g[1, 0], ln_b[1, 0], n_ctx_rows=0, h_row_offset=C,
                      gate=GATE_MIX, shift=SHIFT_FFN, scale=SCALE_FFN, f_dtype=F32)

    f2 = f.reshape(B * T, D)
    idx, wts = _router(f2, moe_router[0].T)
    slot_token, pair_slot, tiles = _routing_tables(idx, n_experts)
    xs = _gather_rows(f2, slot_token, tiles[1], BF16)
    act = _expert_up(xs, moe_w_gu, 0, tiles)
    ys = _expert_down(act, moe_w_down, 0, tiles)
    return _combine_ln(h3, ys, pair_slot, wts, mods[1], ln_g[1, 1], ln_b[1, 1], gate=GATE_FFN)
```

```python
import functools
import math

import jax
import jax.numpy as jnp
from jax import lax
from jax.experimental import pallas as pl
from jax.experimental.pallas import tpu as pltpu

F32 = jnp.float32
BF16 = jnp.bfloat16

GRID_W = 64
ML_HEADS = 8
ATT_HEADS = 32
ATT_KV = 8
WINDOW = 128
Q_BLOCK = 128
ROPE_THETA = 10000.0
TOP_K = 2
LN_EPS = 1e-5
RMS_EPS = 1e-6
DEPTH = 2
ALPHA = (2 * DEPTH) ** 0.25

V7X_VMEM_BYTES = 64 * 1024 * 1024
VMEM_CAP = V7X_VMEM_BYTES - 8 * 1024 * 1024
LANES = 128
MASKED = -1e30

ROW_TILE = 256
MM_ROWS = 1088
ML_CHUNK = 256
MOE_ROWS = 1024
MOE_PARTS = 4


def _divisor(n, target, mult):
    best = None
    for d in range(mult, min(n, target) + 1, mult):
        if n % d == 0:
            best = d
    return best if best is not None else n


def _params(sem, vmem_bytes):
    limit = int(min(VMEM_CAP, max(32 * 1024 * 1024, vmem_bytes)))
    return pltpu.CompilerParams(dimension_semantics=sem, vmem_limit_bytes=limit)


def _nbytes(shape, dtype):
    return math.prod(shape) * jnp.dtype(dtype).itemsize


def _mm_kernel(a_ref, w_ref, o_ref, *, w_transposed):
    contract = (((1,), (1 if w_transposed else 0,)), ((), ()))
    o_ref[...] = lax.dot_general(a_ref[...], w_ref[...].astype(BF16), contract,
                                 preferred_element_type=F32).astype(o_ref.dtype)


def _matmul(a, w, layer, *, n_out=None, out_dtype=F32, tn=512, a_buffers=2, w_transposed=False, name="matmul"):
    M, K = a.shape
    n_out = w.shape[1 if w_transposed else 2] if n_out is None else n_out
    tm = _divisor(M, MM_ROWS, 16)
    tn = _divisor(n_out, tn, LANES)
    vmem = (a_buffers * _nbytes((tm, K), BF16) + 2 * _nbytes((K, tn), w.dtype) + _nbytes((K, tn), BF16)
            + 2 * _nbytes((tm, tn), out_dtype) + _nbytes((tm, tn), F32) + (4 << 20))
    a_mode = {} if a_buffers == 2 else {"pipeline_mode": pl.Buffered(a_buffers)}
    w_spec = (pl.BlockSpec((None, tn, K), lambda i, j: (layer, j, 0)) if w_transposed
              else pl.BlockSpec((None, K, tn), lambda i, j: (layer, 0, j)))
    return pl.pallas_call(
        functools.partial(_mm_kernel, w_transposed=w_transposed),
        out_shape=jax.ShapeDtypeStruct((M, n_out), out_dtype),
        grid=(M // tm, n_out // tn),
        in_specs=[pl.BlockSpec((tm, K), lambda i, j: (i, 0), **a_mode), w_spec],
        out_specs=pl.BlockSpec((tm, tn), lambda i, j: (i, j)),
        compiler_params=_params(("parallel", "arbitrary"), vmem),
        name=name,
    )(a, w)


def _cast_kernel(x_ref, o_ref):
    o_ref[...] = x_ref[...].astype(o_ref.dtype)


def _cast_bf16(w):
    L, K, N = w.shape
    tk = _divisor(K, 512, 16)
    return pl.pallas_call(
        _cast_kernel,
        out_shape=jax.ShapeDtypeStruct((L, K, N), BF16),
        grid=(L, K // tk),
        in_specs=[pl.BlockSpec((None, tk, N), lambda l, k: (l, k, 0))],
        out_specs=pl.BlockSpec((None, tk, N), lambda l, k: (l, k, 0)),
        compiler_params=_params(("parallel", "arbitrary"), 48 << 20),
        name="cast_bf16",
    )(w)


def _swiglu_up_kernel(a_ref, wg_ref, wu_ref, o_ref):
    a = a_ref[...]
    g = jnp.dot(a, wg_ref[...].astype(BF16), preferred_element_type=F32)
    u = jnp.dot(a, wu_ref[...].astype(BF16), preferred_element_type=F32)
    o_ref[...] = (g * jax.nn.sigmoid(g) * u).astype(o_ref.dtype)


def _swiglu_up(a, w_gu, layer, *, tn=256, name="swiglu_up"):
    M, K = a.shape
    F = w_gu.shape[2] // 2
    tm = _divisor(M, MM_ROWS, 16)
    tn = _divisor(F, tn, LANES)
    nf = F // tn
    vmem = (2 * _nbytes((tm, K), BF16) + 4 * _nbytes((K, tn), F32) + 2 * _nbytes((K, tn), BF16)
            + 2 * _nbytes((tm, tn), BF16) + 3 * _nbytes((tm, tn), F32) + (4 << 20))
    return pl.pallas_call(
        _swiglu_up_kernel,
        out_shape=jax.ShapeDtypeStruct((M, F), BF16),
        grid=(M // tm, nf),
        in_specs=[pl.BlockSpec((tm, K), lambda i, j: (i, 0)),
                  pl.BlockSpec((None, K, tn), lambda i, j: (layer, 0, j)),
                  pl.BlockSpec((None, K, tn), lambda i, j: (layer, 0, j + nf))],
        out_specs=pl.BlockSpec((tm, tn), lambda i, j: (i, j)),
        compiler_params=_params(("parallel", "arbitrary"), vmem),
        name=name,
    )(a, w_gu, w_gu)


def _adaln_kernel(c_ref, w_ref, b_ref, o_ref):
    c = c_ref[...]
    a = (c * jax.nn.sigmoid(c)).astype(BF16)
    o_ref[...] = jnp.dot(a, w_ref[...].astype(BF16), preferred_element_type=F32) + b_ref[...]


def _adaln(cond, w, b, layer, *, tn=512):
    R, D = cond.shape
    N = w.shape[2]
    tn = _divisor(N, tn, LANES)
    vmem = 2 * _nbytes((D, tn), F32) + _nbytes((D, tn), BF16) + (4 << 20)
    out = pl.pallas_call(
        _adaln_kernel,
        out_shape=jax.ShapeDtypeStruct((R, N), F32),
        grid=(N // tn,),
        in_specs=[pl.BlockSpec((R, D), lambda j: (0, 0)),
                  pl.BlockSpec((None, D, tn), lambda j: (layer, 0, j)),
                  pl.BlockSpec((None, 1, tn), lambda j: (layer, 0, j))],
        out_specs=pl.BlockSpec((R, tn), lambda j: (0, j)),
        compiler_params=_params(("arbitrary",), vmem),
        name="adaln",
    )(cond, w, b.reshape(b.shape[0], 1, N))
    return out.reshape(R, 6, D)


def _stream_specs(n_ctx_tiles, tr, D):
    ctx = pl.BlockSpec((1, tr, D), lambda b, r: (b, jnp.minimum(r, n_ctx_tiles - 1), 0))
    lat = pl.BlockSpec((1, tr, D), lambda b, r: (b, jnp.maximum(r - n_ctx_tiles, 0), 0))
    return ctx, lat


def _modulate_kernel(c_ref, x_ref, m_ref, o_ref, *, n_ctx_tiles, shift, scale):
    m = m_ref[0]

    def emit(h_ref):
        o_ref[0] = (h_ref[0] * (1.0 + m[scale:scale + 1]) + m[shift:shift + 1]).astype(o_ref.dtype)

    pl.when(pl.program_id(1) < n_ctx_tiles)(lambda: emit(c_ref))
    pl.when(pl.program_id(1) >= n_ctx_tiles)(lambda: emit(x_ref))


def _modulate(ctx, x, mods, *, shift, scale):
    B, C, D = ctx.shape
    S = C + x.shape[1]
    tr = _divisor(math.gcd(S, C), ROW_TILE, 16)
    nct = C // tr
    return pl.pallas_call(
        functools.partial(_modulate_kernel, n_ctx_tiles=nct, shift=shift, scale=scale),
        out_shape=jax.ShapeDtypeStruct((B, S, D), BF16),
        grid=(B, S // tr),
        in_specs=[*_stream_specs(nct, tr, D),
                  pl.BlockSpec((1, 6, D), lambda b, r: (jnp.where(r < nct, B, b), 0, 0))],
        out_specs=pl.BlockSpec((1, tr, D), lambda b, r: (b, r, 0)),
        compiler_params=_params(("parallel", "arbitrary"), 0),
        name="modulate",
    )(ctx, x, mods)


def _layer_norm_rows(z, g, b):
    mu = jnp.mean(z, axis=-1, keepdims=True)
    zc = z - mu
    var = jnp.mean(zc * zc, axis=-1, keepdims=True)
    return zc * lax.rsqrt(var + LN_EPS) * g + b


def _resid_ln_kernel(*refs, n_ctx_tiles, split_h, gate, shift, scale):
    h_refs, (y_ref, m_ref, mn_ref, g_ref, b_ref, hn_ref, f_ref) = refs[:-7], refs[-7:]
    m = m_ref[0]
    mn = mn_ref[0]

    def emit(h_ref):
        hn = _layer_norm_rows(ALPHA * h_ref[0] + m[gate:gate + 1] * y_ref[0], g_ref[...], b_ref[...])
        hn_ref[0] = hn
        f_ref[0] = (hn * (1.0 + mn[scale:scale + 1]) + mn[shift:shift + 1]).astype(f_ref.dtype)

    if split_h:
        pl.when(pl.program_id(1) < n_ctx_tiles)(lambda: emit(h_refs[0]))
        pl.when(pl.program_id(1) >= n_ctx_tiles)(lambda: emit(h_refs[1]))
    else:
        emit(h_refs[0])


def _resid_ln(h, y, mods, mods_next, ln_g, ln_b, *, n_ctx_rows, h_row_offset, gate, shift, scale, f_dtype=BF16):
    B, R, D = y.shape
    tr = _divisor(math.gcd(R, n_ctx_rows, h_row_offset), ROW_TILE, 16)
    nct = n_ctx_rows // tr
    off = h_row_offset // tr
    cond = lambda b, r: (jnp.where(r < nct, B, b), 0, 0)
    row = pl.BlockSpec((1, tr, D), lambda b, r: (b, r, 0))
    vec = pl.BlockSpec((1, D), lambda b, r: (0, 0))
    split_h = isinstance(h, tuple)
    if split_h:
        assert off == 0 and nct > 0
        h_arrays, h_specs = list(h), list(_stream_specs(nct, tr, D))
    else:
        h_arrays, h_specs = [h], [pl.BlockSpec((1, tr, D), lambda b, r: (b, r + off, 0))]
    return pl.pallas_call(
        functools.partial(_resid_ln_kernel, n_ctx_tiles=nct, split_h=split_h, gate=gate, shift=shift, scale=scale),
        out_shape=(jax.ShapeDtypeStruct((B, R, D), F32), jax.ShapeDtypeStruct((B, R, D), f_dtype)),
        grid=(B, R // tr),
        in_specs=[*h_specs, row, pl.BlockSpec((1, 6, D), cond), pl.BlockSpec((1, 6, D), cond), vec, vec],
        out_specs=(row, row),
        compiler_params=_params(("parallel", "arbitrary"), 48 << 20),
        name="resid_ln",
    )(*h_arrays, y, mods, mods_next, ln_g.reshape(1, D), ln_b.reshape(1, D))


def _gates_kernel(w_ref, a_ref, b_ref, o_ref):
    g = lax.dot_general(w_ref[...].astype(BF16), a_ref[0], (((1,), (1,)), ((), ())),
                        preferred_element_type=F32)
    o_ref[0] = g + b_ref[...]


def _mlstm_gates(a, w_t, layer, gate_b, *, first_row):
    B, S, D = a.shape
    G = gate_b.shape[0]
    assert first_row % G == 0
    ts = _divisor(S, 1088, LANES)
    return pl.pallas_call(
        _gates_kernel,
        out_shape=jax.ShapeDtypeStruct((B, G, S), F32),
        grid=(B, S // ts),
        in_specs=[pl.BlockSpec((None, G, D), lambda b, s: (layer, first_row // G, 0)),
                  pl.BlockSpec((1, ts, D), lambda b, s: (b, s, 0)),
                  pl.BlockSpec((G, 1), lambda b, s: (0, 0))],
        out_specs=pl.BlockSpec((1, G, ts), lambda b, s: (b, 0, s)),
        compiler_params=_params(("parallel", "arbitrary"), 0),
        name="mlstm_gates",
    )(w_t, a, gate_b.reshape(G, 1))


def _log_sigmoid(x):
    return jnp.minimum(x, 0.0) - jnp.log1p(jnp.exp(-jnp.abs(x)))


def _mlstm_scan_kernel(g_ref, q_ref, k_ref, v_ref, h_ref, c_ref, n_ref, m_ref, *, heads, dk, dv, chunk):
    direction = pl.program_id(1)
    step = pl.program_id(2)
    L = chunk

    @pl.when(step == 0)
    def _():
        c_ref[...] = jnp.zeros_like(c_ref)
        n_ref[...] = jnp.zeros_like(n_ref)
        m_ref[...] = jnp.zeros_like(m_ref)

    row = lax.broadcasted_iota(jnp.int32, (L, L), 0)
    col = lax.broadcasted_iota(jnp.int32, (L, L), 1)
    seen = jnp.where(direction == 0, col - row, row - col) <= 0
    eye = row == col

    def to_col(r):
        return jnp.sum(jnp.where(eye, r, 0.0), axis=1, keepdims=True)

    def to_row(c):
        return jnp.sum(jnp.where(eye, c, 0.0), axis=0, keepdims=True)

    for h in range(heads):
        ig = g_ref[0, pl.ds(direction * 2 * heads + h, 1), :]
        lf = _log_sigmoid(g_ref[0, pl.ds(direction * 2 * heads + heads + h, 1), :])
        q = q_ref[0, :, h * dk:(h + 1) * dk] * (dk ** -0.5)
        k = k_ref[0, :, h * dk:(h + 1) * dk]
        v = v_ref[0, :, h * dv:(h + 1) * dv]
        m_prev = m_ref[h]

        b_col = jnp.sum(jnp.where(seen, lf, 0.0), axis=1, keepdims=True)
        b_row = to_row(b_col)
        d = jnp.where(seen, b_col - b_row + ig, MASKED)
        inter = b_col + m_prev
        m_t = jnp.maximum(inter, jnp.max(d, axis=1, keepdims=True))
        s = lax.dot_general(q, k, (((1,), (1,)), ((), ())), preferred_element_type=F32) * jnp.exp(d - m_t)
        g_in = jnp.exp(inter - m_t)
        num = (g_in * jnp.dot(q, c_ref[h].astype(BF16), preferred_element_type=F32)
               + jnp.dot(s.astype(BF16), v, preferred_element_type=F32))
        qf = q.astype(F32)
        den = g_in * jnp.sum(qf * n_ref[h], axis=1, keepdims=True) + jnp.sum(s, axis=1, keepdims=True)
        h_ref[0, 0, :, h * dv:(h + 1) * dv] = num / jnp.maximum(jnp.abs(den), jnp.exp(-m_t))

        b_last = jnp.sum(lf, axis=1, keepdims=True)
        w_row = b_last - b_row + ig
        m_new = jnp.maximum(b_last + m_prev, jnp.max(w_row, axis=1, keepdims=True))
        decay = jnp.exp(b_last + m_prev - m_new)
        kw = k.astype(F32) * to_col(jnp.exp(w_row - m_new))
        c_ref[h] = decay * c_ref[h] + jnp.dot(kw.T.astype(BF16), v, preferred_element_type=F32)
        n_ref[h] = decay * n_ref[h] + jnp.sum(kw, axis=0, keepdims=True)
        m_ref[h] = m_new


def _mlstm_scan(p, gates, *, n_ctx_rows, heads, qk, vw):
    B, S, _ = p.shape
    L = _divisor(math.gcd(S, n_ctx_rows), ML_CHUNK, LANES)
    nc, ncc = S // L, n_ctx_rows // L
    dk, dv = qk // heads, vw // heads

    def chunk(d, j):
        back = jnp.where(j < ncc, ncc - 1 - j, nc - 1 - (j - ncc))
        return jnp.where(d == 0, j, back)

    vmem = (2 * (2 * _nbytes((L, qk), BF16) + _nbytes((L, vw), BF16) + _nbytes((L, vw), F32))
            + _nbytes((heads, dk, dv), F32) + (16 << 20))
    return pl.pallas_call(
        functools.partial(_mlstm_scan_kernel, heads=heads, dk=dk, dv=dv, chunk=L),
        out_shape=jax.ShapeDtypeStruct((2, B, S, vw), F32),
        grid=(B, 2, nc),
        in_specs=[pl.BlockSpec((1, 4 * heads, L), lambda b, d, j: (b, 0, chunk(d, j))),
                  pl.BlockSpec((1, L, qk), lambda b, d, j: (b, chunk(d, j), 0)),
                  pl.BlockSpec((1, L, qk), lambda b, d, j: (b, chunk(d, j), 1)),
                  pl.BlockSpec((1, L, vw), lambda b, d, j: (b, chunk(d, j), (2 * qk) // vw))],
        out_specs=pl.BlockSpec((1, 1, L, vw), lambda b, d, j: (d, b, chunk(d, j), 0)),
        scratch_shapes=[pltpu.VMEM((heads, dk, dv), F32), pltpu.VMEM((heads, 1, dk), F32),
                        pltpu.VMEM((heads, 1, 1), F32)],
        compiler_params=_params(("parallel", "parallel", "arbitrary"), vmem),
        name="mlstm_scan",
    )(gates, p, p, p)


def _mlstm_norm_kernel(h_ref, o_ref, g_ref, out_ref, *, heads, dv):
    hs = h_ref[0, 0] + h_ref[1, 0]
    for h in range(heads):
        x = hs[:, h * dv:(h + 1) * dv]
        x = x * lax.rsqrt(jnp.mean(x * x, axis=-1, keepdims=True) + RMS_EPS)
        x = x * g_ref[:, h * dv:(h + 1) * dv] * jax.nn.sigmoid(o_ref[0, :, h * dv:(h + 1) * dv].astype(F32))
        out_ref[0, :, h * dv:(h + 1) * dv] = x.astype(out_ref.dtype)


def _mlstm_norm(hfb, p, norm_g, *, heads, o_col_block):
    _, B, S, vw = hfb.shape
    tr = _divisor(S, ROW_TILE, 16)
    return pl.pallas_call(
        functools.partial(_mlstm_norm_kernel, heads=heads, dv=vw // heads),
        out_shape=jax.ShapeDtypeStruct((B, S, vw), BF16),
        grid=(B, S // tr),
        in_specs=[pl.BlockSpec((2, 1, tr, vw), lambda b, r: (0, b, r, 0)),
                  pl.BlockSpec((1, tr, vw), lambda b, r: (b, r, o_col_block)),
                  pl.BlockSpec((1, vw), lambda b, r: (0, 0))],
        out_specs=pl.BlockSpec((1, tr, vw), lambda b, r: (b, r, 0)),
        compiler_params=_params(("parallel", "arbitrary"), 48 << 20),
        name="mlstm_norm",
    )(hfb, p, norm_g.reshape(1, vw))


def _qkv_rope_kernel(a_ref, w_ref, cos_ref, sin_ref, o_ref, *, rope_blocks):
    j = pl.program_id(1)
    acc = jnp.dot(a_ref[...], w_ref[...].astype(BF16), preferred_element_type=F32)

    @pl.when(j < rope_blocks)
    def _():
        tn = acc.shape[1]
        lane = lax.broadcasted_iota(jnp.int32, (1, LANES), 1)
        first = (lane // (LANES // 4)) % 2 == 0
        for hd in range(tn // LANES):
            x = acc[:, hd * LANES:(hd + 1) * LANES]
            partner = jnp.where(first, pltpu.roll(x, LANES - LANES // 4, 1), pltpu.roll(x, LANES // 4, 1))
            o_ref[:, hd * LANES:(hd + 1) * LANES] = (x * cos_ref[...] + partner * sin_ref[...]).astype(o_ref.dtype)

    @pl.when(j >= rope_blocks)
    def _():
        o_ref[...] = acc.astype(o_ref.dtype)


def _qkv_rope(a, w, layer, cos, sin, *, rope_cols, rows_per_batch, tn=512):
    M, K = a.shape
    N = w.shape[2]
    tm = _divisor(rows_per_batch, MM_ROWS, 16)
    tn = _divisor(math.gcd(N, rope_cols), tn, LANES)
    ntab = rows_per_batch // tm
    vmem = (2 * _nbytes((tm, K), BF16) + 2 * _nbytes((K, tn), F32) + _nbytes((K, tn), BF16)
            + 2 * _nbytes((tm, tn), BF16) + 3 * _nbytes((tm, tn), F32) + (6 << 20))
    return pl.pallas_call(
        functools.partial(_qkv_rope_kernel, rope_blocks=rope_cols // tn),
        out_shape=jax.ShapeDtypeStruct((M, N), BF16),
        grid=(M // tm, N // tn),
        in_specs=[pl.BlockSpec((tm, K), lambda i, j: (i, 0)),
                  pl.BlockSpec((None, K, tn), lambda i, j: (layer, 0, j)),
                  pl.BlockSpec((tm, LANES), lambda i, j: (i % ntab, 0)),
                  pl.BlockSpec((tm, LANES), lambda i, j: (i % ntab, 0))],
        out_specs=pl.BlockSpec((tm, tn), lambda i, j: (i, j)),
        compiler_params=_params(("parallel", "arbitrary"), vmem),
        name="qkv_rope",
    )(a, w, cos, sin)


def _attention_kernel(sink_ref, q_ref, kp_ref, kc_ref, kn_ref, kx_ref, vp_ref, vc_ref, vn_ref, vx_ref, o_ref,
                      *, group, kv_step, n_blocks, scale):
    n = pl.program_id(2)
    T = q_ref.shape[1]
    row = lax.broadcasted_iota(jnp.int32, (group * T, T), 0) % T
    col = lax.broadcasted_iota(jnp.int32, (group * T, T), 1)
    prev_ok = (col >= row) & (n > 0)
    next_ok = (col <= row) & (n < n_blocks - 1)
    for h in range(kv_step):
        kv = pl.program_id(1) * kv_step + h
        hs = slice(h * LANES, (h + 1) * LANES)
        q = jnp.concatenate([q_ref[0, :, (h * group + g) * LANES:(h * group + g + 1) * LANES] for g in range(group)],
                            axis=0)
        k = jnp.concatenate([kp_ref[0, :, hs], kc_ref[0, :, hs], kn_ref[0, :, hs], kx_ref[0, :, hs]], axis=0)
        v = jnp.concatenate([vp_ref[0, :, hs], vc_ref[0, :, hs], vn_ref[0, :, hs], vx_ref[0, :, hs]], axis=0)
        s = lax.dot_general(q, k, (((1,), (1,)), ((), ())), preferred_element_type=F32) * scale
        s_p = jnp.where(prev_ok, s[:, :T], MASKED)
        s_c = s[:, T:2 * T]
        s_n = jnp.where(next_ok, s[:, 2 * T:3 * T], MASKED)
        s_x = s[:, 3 * T:]
        sink = jnp.concatenate([jnp.full((T, 1), sink_ref[kv * group + g], F32) for g in range(group)], axis=0)
        m = jnp.maximum(jnp.maximum(jnp.max(s_p, axis=1, keepdims=True), jnp.max(s_c, axis=1, keepdims=True)),
                        jnp.maximum(jnp.max(s_n, axis=1, keepdims=True), jnp.max(s_x, axis=1, keepdims=True)))
        m = jnp.maximum(m, sink)
        e_p, e_c, e_n, e_x = jnp.exp(s_p - m), jnp.exp(s_c - m), jnp.exp(s_n - m), jnp.exp(s_x - m)
        total = (jnp.sum(e_p, axis=1, keepdims=True) + jnp.sum(e_c, axis=1, keepdims=True)
                 + jnp.sum(e_n, axis=1, keepdims=True) + jnp.sum(e_x, axis=1, keepdims=True) + jnp.exp(sink - m))
        inv = 1.0 / total
        p = jnp.concatenate([(e_p * inv).astype(BF16), (e_c * inv).astype(BF16), (e_n * inv).astype(BF16),
                             (e_x * inv).astype(BF16)], axis=1)
        o = jnp.dot(p, v, preferred_element_type=F32)
        for g in range(group):
            o_ref[0, :, (h * group + g) * LANES:(h * group + g + 1) * LANES] = o[g * T:(g + 1) * T].astype(o_ref.dtype)


def _window_attention(p, sink, *, n_ctx_rows, heads, kv_heads):
    B, S, _ = p.shape
    T = S - n_ctx_rows
    assert WINDOW == Q_BLOCK and T % Q_BLOCK == 0 and n_ctx_rows % Q_BLOCK == 0
    nb = T // Q_BLOCK
    cb = n_ctx_rows // Q_BLOCK
    group = heads // kv_heads
    kv_step = 2 if kv_heads % 2 == 0 else 1
    kw = kv_step * LANES
    kcol, vcol = heads // kv_step, (heads + kv_heads) // kv_step

    def blk(colbase, shift):
        def index(b, kv, n):
            return (b, jnp.clip(n + shift, 0, nb - 1) + cb, colbase + kv)
        return pl.BlockSpec((1, Q_BLOCK, kw), index)

    def ctx(colbase):
        return pl.BlockSpec((1, n_ctx_rows, kw), lambda b, kv, n: (b, 0, colbase + kv))

    qw = kv_step * group * LANES
    return pl.pallas_call(
        functools.partial(_attention_kernel, group=group, kv_step=kv_step, n_blocks=nb, scale=LANES ** -0.5),
        out_shape=jax.ShapeDtypeStruct((B, T, heads * LANES), BF16),
        grid=(B, kv_heads // kv_step, nb),
        in_specs=[pl.BlockSpec(memory_space=pltpu.SMEM),
                  pl.BlockSpec((1, Q_BLOCK, qw), lambda b, kv, n: (b, n + cb, kv)),
                  blk(kcol, -1), blk(kcol, 0), blk(kcol, 1), ctx(kcol),
                  blk(vcol, -1), blk(vcol, 0), blk(vcol, 1), ctx(vcol)],
        out_specs=pl.BlockSpec((1, Q_BLOCK, qw), lambda b, kv, n: (b, n, kv)),
        compiler_params=_params(("parallel", "parallel", "arbitrary"), 0),
        name="window_attention",
    )(sink, p, p, p, p, p, p, p, p, p)


def _rope_tables(n_ctx_rows, seq):
    quarter = LANES // 4
    freqs = ROPE_THETA ** (-jnp.arange(quarter, dtype=F32) / quarter)
    t = jnp.arange(seq)
    ang_r = (t // GRID_W).astype(F32)[:, None] * freqs[None, :]
    ang_c = (t % GRID_W).astype(F32)[:, None] * freqs[None, :]
    cos = jnp.concatenate([jnp.cos(ang_r), jnp.cos(ang_r), jnp.cos(ang_c), jnp.cos(ang_c)], axis=-1)
    sin = jnp.concatenate([-jnp.sin(ang_r), jnp.sin(ang_r), -jnp.sin(ang_c), jnp.sin(ang_c)], axis=-1)
    cos = jnp.concatenate([jnp.ones((n_ctx_rows, LANES), F32), cos], axis=0)
    sin = jnp.concatenate([jnp.zeros((n_ctx_rows, LANES), F32), sin], axis=0)
    return cos, sin


def _router_kernel(w_ref, f_ref, idx_ref, wt_ref):
    logits = lax.dot_general(w_ref[...].astype(BF16), f_ref[...].astype(BF16), (((1,), (1,)), ((), ())),
                             preferred_element_type=F32)
    n_exp = logits.shape[0]
    eid = lax.broadcasted_iota(jnp.int32, logits.shape, 0)
    v1 = jnp.max(logits, axis=0, keepdims=True)
    i1 = jnp.min(jnp.where(logits == v1, eid, n_exp), axis=0, keepdims=True)
    rest = jnp.where(eid == i1, -jnp.inf, logits)
    v2 = jnp.max(rest, axis=0, keepdims=True)
    i2 = jnp.min(jnp.where(rest == v2, eid, n_exp), axis=0, keepdims=True)
    e2 = jnp.exp(v2 - v1)
    idx_ref[0:1, :] = i1
    idx_ref[1:2, :] = i2
    wt_ref[0:1, :] = 1.0 / (1.0 + e2)
    wt_ref[1:2, :] = e2 / (1.0 + e2)


def _router(f, w_router_t):
    T, D = f.shape
    E = w_router_t.shape[0]
    tt = _divisor(T, 1024, LANES)
    return pl.pallas_call(
        _router_kernel,
        out_shape=(jax.ShapeDtypeStruct((TOP_K, T), jnp.int32), jax.ShapeDtypeStruct((TOP_K, T), F32)),
        grid=(T // tt,),
        in_specs=[pl.BlockSpec((E, D), lambda i: (0, 0)), pl.BlockSpec((tt, D), lambda i: (i, 0))],
        out_specs=(pl.BlockSpec((TOP_K, tt), lambda i: (0, i)), pl.BlockSpec((TOP_K, tt), lambda i: (0, i))),
        compiler_params=_params(("arbitrary",), 48 << 20),
        name="moe_router",
    )(w_router_t, f)


GATHER_ROWS = MOE_ROWS // MOE_PARTS
GATHER_UNROLL = 8


def _gather_kernel(idx_ref, tp_ref, src_ref, o_ref, stage, sem):
    i = pl.program_id(0)
    base = i * GATHER_ROWS
    used = (i % MOE_PARTS) < tp_ref[i // MOE_PARTS]

    @pl.when(used)
    def _():
        def issue(r, carry):
            pltpu.make_async_copy(src_ref.at[pl.ds(idx_ref[base + r], 1), :], stage.at[pl.ds(r, 1), :], sem).start()
            return carry

        lax.fori_loop(0, GATHER_ROWS, issue, 0, unroll=GATHER_UNROLL)
        pltpu.make_async_copy(src_ref.at[pl.ds(0, GATHER_ROWS), :], stage, sem).wait()
        o_ref[...] = stage[...].astype(o_ref.dtype)

    @pl.when(jnp.logical_not(used))
    def _():
        o_ref[...] = jnp.zeros_like(o_ref)


def _gather_rows(src, idx, tile_parts, out_dtype):
    N, D = src.shape
    R = idx.shape[0]
    assert R % MOE_ROWS == 0 and D % LANES == 0
    return pl.pallas_call(
        _gather_kernel,
        out_shape=jax.ShapeDtypeStruct((R, D), out_dtype),
        grid_spec=pltpu.PrefetchScalarGridSpec(
            num_scalar_prefetch=2,
            grid=(R // GATHER_ROWS,),
            in_specs=[pl.BlockSpec(memory_space=pl.ANY)],
            out_specs=pl.BlockSpec((GATHER_ROWS, D), lambda i, idx_ref, tp_ref: (i, 0)),
            scratch_shapes=[pltpu.VMEM((GATHER_ROWS, D), src.dtype), pltpu.SemaphoreType.DMA(())],
        ),
        compiler_params=_params(("arbitrary",), 0),
        name="gather_rows",
    )(idx, tile_parts, src)


def _for_used_parts(parts, o_ref, emit):
    part = o_ref.shape[0] // MOE_PARTS
    for n in range(MOE_PARTS + 1):
        @pl.when(parts == n)
        def _(n=n):
            if n > 0:
                emit(slice(0, n * part))
            if n < MOE_PARTS:
                o_ref[n * part:, :] = jnp.zeros(((MOE_PARTS - n) * part, o_ref.shape[1]), o_ref.dtype)


def _expert_up_kernel(te_ref, tp_ref, ts_ref, x_ref, wg_ref, wu_ref, o_ref, wgb_ref, wub_ref):
    i = pl.program_id(1)
    fresh = jnp.logical_or(i == 0, te_ref[i] != te_ref[jnp.maximum(i - 1, 0)])

    @pl.when(fresh)
    def _():
        wgb_ref[...] = wg_ref[...].astype(BF16)
        wub_ref[...] = wu_ref[...].astype(BF16)

    def emit(rows):
        x = x_ref[rows, :]
        g = jnp.dot(x, wgb_ref[...], preferred_element_type=F32)
        u = jnp.dot(x, wub_ref[...], preferred_element_type=F32)
        o_ref[rows, :] = (g * jax.nn.sigmoid(g) * u).astype(o_ref.dtype)

    _for_used_parts(tp_ref[i], o_ref, emit)


def _expert_up(xs, w_gu, layer, tiles, *, tn=256):
    P, D = xs.shape
    F2 = w_gu.shape[3]
    F = F2 // 2
    tm = MOE_ROWS
    tn = _divisor(F, tn, LANES)
    nf = F // tn
    vmem = (2 * _nbytes((tm, D), BF16) + 4 * _nbytes((D, tn), F32) + 2 * _nbytes((D, tn), BF16)
            + 2 * _nbytes((tm, tn), BF16) + 3 * _nbytes((tm, tn), F32) + (4 << 20))
    return pl.pallas_call(
        _expert_up_kernel,
        out_shape=jax.ShapeDtypeStruct((P, F), BF16),
        grid_spec=pltpu.PrefetchScalarGridSpec(
            num_scalar_prefetch=3,
            grid=(nf, P // tm),
            in_specs=[pl.BlockSpec((tm, D), lambda j, i, te, th, ts: (ts[i], 0)),
                      pl.BlockSpec((None, None, D, tn), lambda j, i, te, th, ts: (layer, te[i], 0, j)),
                      pl.BlockSpec((None, None, D, tn), lambda j, i, te, th, ts: (layer, te[i], 0, j + nf))],
            out_specs=pl.BlockSpec((tm, tn), lambda j, i, te, th, ts: (i, j)),
            scratch_shapes=[pltpu.VMEM((D, tn), BF16), pltpu.VMEM((D, tn), BF16)],
        ),
        compiler_params=_params(("arbitrary", "arbitrary"), vmem),
        name="expert_up",
    )(*tiles, xs, w_gu, w_gu)


def _expert_down_kernel(te_ref, tp_ref, ts_ref, a_ref, w_ref, o_ref, wb_ref):
    i = pl.program_id(1)
    fresh = jnp.logical_or(i == 0, te_ref[i] != te_ref[jnp.maximum(i - 1, 0)])

    @pl.when(fresh)
    def _():
        wb_ref[...] = w_ref[...].astype(BF16)

    def emit(rows):
        o_ref[rows, :] = jnp.dot(a_ref[rows, :], wb_ref[...], preferred_element_type=F32)

    _for_used_parts(tp_ref[i], o_ref, emit)


def _expert_down(act, w_down, layer, tiles, *, tn=512):
    P, F = act.shape
    D = w_down.shape[3]
    tm = MOE_ROWS
    tn = _divisor(D, tn, LANES)
    vmem = (2 * _nbytes((tm, F), BF16) + 2 * _nbytes((F, tn), F32) + _nbytes((F, tn), BF16)
            + 3 * _nbytes((tm, tn), F32) + (4 << 20))
    return pl.pallas_call(
        _expert_down_kernel,
        out_shape=jax.ShapeDtypeStruct((P, D), F32),
        grid_spec=pltpu.PrefetchScalarGridSpec(
            num_scalar_prefetch=3,
            grid=(D // tn, P // tm),
            in_specs=[pl.BlockSpec((tm, F), lambda j, i, te, th, ts: (ts[i], 0)),
                      pl.BlockSpec((None, None, F, tn), lambda j, i, te, th, ts: (layer, te[i], 0, j))],
            out_specs=pl.BlockSpec((tm, tn), lambda j, i, te, th, ts: (i, j)),
            scratch_shapes=[pltpu.VMEM((F, tn), BF16)],
        ),
        compiler_params=_params(("arbitrary", "arbitrary"), vmem),
        name="expert_down",
    )(*tiles, act, w_down)


def _routing_tables(idx, n_experts):
    K, T = idx.shape
    tm = MOE_ROWS
    n_tiles = (K * T) // tm + n_experts
    P = n_tiles * tm
    e = idx.reshape(-1)
    onehot = (e[:, None] == jnp.arange(n_experts)[None, :]).astype(jnp.int32)
    csum = jnp.cumsum(onehot, axis=0)
    rank = jnp.sum(csum * onehot, axis=1) - 1
    counts = csum[-1]
    tiles_per = (counts + tm - 1) // tm
    tile_end = jnp.cumsum(tiles_per)
    start = (tile_end - tiles_per) * tm
    slot = jnp.sum(onehot * start[None, :], axis=1) + rank
    token = jnp.tile(jnp.arange(T, dtype=jnp.int32), K)
    slot_token = jnp.zeros((P,), jnp.int32).at[slot].set(token)
    tiles = jnp.arange(n_tiles, dtype=jnp.int32)
    owner = jnp.sum((tile_end[None, :] <= tiles[:, None]).astype(jnp.int32), axis=1)
    owns = (owner[:, None] == jnp.arange(n_experts)[None, :]).astype(jnp.int32)
    rows_used = jnp.clip(jnp.sum(owns * (start + counts)[None, :], axis=1) - tiles * tm, 0, tm)
    part = tm // MOE_PARTS
    tile_parts = ((rows_used + part - 1) // part).astype(jnp.int32)
    last_owner = jnp.sum((tile_end <= tile_end[-1] - 1).astype(jnp.int32))
    tile_expert = jnp.where(rows_used > 0, owner, last_owner).astype(jnp.int32)
    tile_src = jnp.minimum(tiles, tile_end[-1] - 1).astype(jnp.int32)
    return slot_token, slot.reshape(K, T).astype(jnp.int32), (tile_expert, tile_parts, tile_src)


def _combine_ln_kernel(slot_ref, ys_ref, h_ref, w_ref, m_ref, g_ref, b_ref, o_ref, ybuf, sems, *, gate, n_tokens):
    s = pl.program_id(0)
    tr = o_ref.shape[0]

    def issue(step, buf):
        def body(r, carry):
            for k in range(TOP_K):
                row = slot_ref[k * n_tokens + step * tr + r]
                pltpu.make_async_copy(ys_ref.at[pl.ds(row, 1), :], ybuf.at[buf, k, pl.ds(r, 1), :], sems.at[buf]).start()
            return carry
        lax.fori_loop(0, tr, body, 0, unroll=GATHER_UNROLL // TOP_K)

    @pl.when(s == 0)
    def _():
        issue(0, 0)

    @pl.when(s + 1 < pl.num_programs(0))
    def _():
        issue(s + 1, (s + 1) % 2)

    buf = s % 2
    for k in range(TOP_K):
        pltpu.make_async_copy(ys_ref.at[pl.ds(0, tr), :], ybuf.at[buf, k], sems.at[buf]).wait()
    y = w_ref[:, 0:1] * ybuf[buf, 0]
    for k in range(1, TOP_K):
        y = y + w_ref[:, k:k + 1] * ybuf[buf, k]
    m = m_ref[0]
    o_ref[...] = _layer_norm_rows(ALPHA * h_ref[...] + m[gate:gate + 1] * y, g_ref[...], b_ref[...])


def _combine_ln(h, ys, pair_slot, wts, mods, ln_g, ln_b, *, gate):
    B, T, D = h.shape
    n_tokens = B * T
    tr = _divisor(T, ROW_TILE, 16)
    row = pl.BlockSpec((tr, D), lambda i, slots: (i, 0))
    vec = pl.BlockSpec((1, D), lambda i, slots: (0, 0))
    out = pl.pallas_call(
        functools.partial(_combine_ln_kernel, gate=gate, n_tokens=n_tokens),
        out_shape=jax.ShapeDtypeStruct((n_tokens, D), F32),
        grid_spec=pltpu.PrefetchScalarGridSpec(
            num_scalar_prefetch=1,
            grid=(n_tokens // tr,),
            in_specs=[pl.BlockSpec(memory_space=pl.ANY), row,
                      pl.BlockSpec((tr, TOP_K), lambda i, slots: (i, 0)),
                      pl.BlockSpec((1, 6, D), lambda i, slots: (i // (T // tr), 0, 0)), vec, vec],
            out_specs=row,
            scratch_shapes=[pltpu.VMEM((2, TOP_K, tr, D), F32), pltpu.SemaphoreType.DMA((2,))],
        ),
        compiler_params=_params(("arbitrary",), 2 * TOP_K * _nbytes((tr, D), F32) + 5 * _nbytes((tr, D), F32) + (4 << 20)),
        name="combine_ln",
    )(pair_slot.reshape(-1), ys, h.reshape(n_tokens, D), wts.T, mods, ln_g.reshape(1, D), ln_b.reshape(1, D))
    return out.reshape(B, T, D)


SHIFT_MIX, SCALE_MIX, GATE_MIX, SHIFT_FFN, SCALE_FFN, GATE_FFN = range(6)


def kernel(x, c, ctx, c_ctx, ada_w, ada_b, ln_g, ln_b, ml_w_in, ml_gate_b, ml_norm_g, ml_w_out, att_w_in, att_sink,
           att_w_out, ffn_w_gu, ffn_w_down, moe_router, moe_w_gu, moe_w_down):
    B, T, D = x.shape
    C = ctx.shape[1]
    S = C + T
    assert ada_w.shape[0] == DEPTH == 2
    qk, vw = ml_w_out.shape[1] // 2, ml_w_out.shape[1]
    n_experts = moe_router.shape[-1]

    cond = jnp.concatenate([c, c_ctx[None, :]], axis=0)
    mods = [_adaln(cond, ada_w, ada_b, i) for i in range(DEPTH)]

    a = _modulate(ctx, x, mods[0], shift=SHIFT_MIX, scale=SCALE_MIX)
    n_main = 2 * qk + 2 * vw
    w_in_t = jnp.swapaxes(ml_w_in, 1, 2)
    p = _matmul(a.reshape(B * S, D), w_in_t, 0, n_out=n_main, out_dtype=BF16, w_transposed=True,
                name="mlstm_in").reshape(B, S, n_main)
    gates = _mlstm_gates(a, w_in_t, 0, ml_gate_b[0], first_row=n_main)
    hfb = _mlstm_scan(p, gates, n_ctx_rows=C, heads=ML_HEADS, qk=qk, vw=vw)
    hn = _mlstm_norm(hfb, p, ml_norm_g[0], heads=ML_HEADS, o_col_block=(2 * qk + vw) // vw)
    y = _matmul(hn.reshape(B * S, vw), ml_w_out, 0, name="mlstm_out").reshape(B, S, D)
    h1, f = _resid_ln((ctx, x), y, mods[0], mods[0], ln_g[0, 0], ln_b[0, 0], n_ctx_rows=C, h_row_offset=0,
                      gate=GATE_MIX, shift=SHIFT_FFN, scale=SCALE_FFN)
    act = _swiglu_up(f.reshape(B * S, D), ffn_w_gu, 0)
    y = _matmul(act, _cast_bf16(ffn_w_down), 0, tn=256, a_buffers=1, name="ffn_down").reshape(B, S, D)
    h2, a = _resid_ln(h1, y, mods[0], mods[1], ln_g[0, 1], ln_b[0, 1], n_ctx_rows=C, h_row_offset=0,
                      gate=GATE_FFN, shift=SHIFT_MIX, scale=SCALE_MIX)

    q_cols = ATT_HEADS * LANES
    cos, sin = _rope_tables(C, T)
    p = _qkv_rope(a.reshape(B * S, D), att_w_in, 0, cos, sin, rope_cols=q_cols + ATT_KV * LANES,
                  rows_per_batch=S).reshape(B, S, -1)
    o = _window_attention(p, att_sink[0], n_ctx_rows=C, heads=ATT_HEADS, kv_heads=ATT_KV)
    y = _matmul(o.reshape(B * T, q_cols), att_w_out, 0, name="att_out").reshape(B, T, D)
    h3, f = _resid_ln(h2, y, mods[1], mods[1], ln_g[1, 0], ln_b[1, 0], n_ctx_rows=0, h_row_offset=C,
                      gate=GATE_MIX, shift=SHIFT_FFN, scale=SCALE_FFN, f_dtype=F32)

    f2 = f.reshape(B * T, D)
    idx, wts = _router(f2, moe_router[0].T)
    slot_token, pair_slot, tiles = _routing_tables(idx, n_experts)
    xs = _gather_rows(f2, slot_token, tiles[1], BF16)
    act = _expert_up(xs, moe_w_gu, 0, tiles)
    ys = _expert_down(act, moe_w_down, 0, tiles)
    return _combine_ln(h3, ys, pair_slot, wts, mods[1], ln_g[1, 1], ln_b[1, 1], gate=GATE_FFN)
```

```python
import functools
import math

import jax
import jax.numpy as jnp
from jax import lax
from jax.experimental import pallas as pl
from jax.experimental.pallas import tpu as pltpu

F32 = jnp.float32
BF16 = jnp.bfloat16

GRID_W = 64
ML_HEADS = 8
ATT_HEADS = 32
ATT_KV = 8
WINDOW = 128
Q_BLOCK = 128
ROPE_THETA = 10000.0
TOP_K = 2
LN_EPS = 1e-5
RMS_EPS = 1e-6
DEPTH = 2
ALPHA = (2 * DEPTH) ** 0.25

V7X_VMEM_BYTES = 64 * 1024 * 1024
VMEM_CAP = V7X_VMEM_BYTES - 8 * 1024 * 1024
LANES = 128
MASKED = -1e30

ROW_TILE = 256
MM_ROWS = 1088
ML_CHUNK = 256
MOE_ROWS = 1024
MOE_PARTS = 4


def _divisor(n, target, mult):
    best = None
    for d in range(mult, min(n, target) + 1, mult):
        if n % d == 0:
            best = d
    return best if best is not None else n


def _params(sem, vmem_bytes):
    limit = int(min(VMEM_CAP, max(32 * 1024 * 1024, vmem_bytes)))
    return pltpu.CompilerParams(dimension_semantics=sem, vmem_limit_bytes=limit)


def _nbytes(shape, dtype):
    return math.prod(shape) * jnp.dtype(dtype).itemsize


def _mm_kernel(a_ref, w_ref, o_ref, *, w_transposed):
    contract = (((1,), (1 if w_transposed else 0,)), ((), ()))
    o_ref[...] = lax.dot_general(a_ref[...], w_ref[...].astype(BF16), contract,
                                 preferred_element_type=F32).astype(o_ref.dtype)


def _matmul(a, w, layer, *, n_out=None, out_dtype=F32, tn=512, a_buffers=2, w_transposed=False, name="matmul"):
    M, K = a.shape
    n_out = w.shape[1 if w_transposed else 2] if n_out is None else n_out
    tm = _divisor(M, MM_ROWS, 16)
    tn = _divisor(n_out, tn, LANES)
    vmem = (a_buffers * _nbytes((tm, K), BF16) + 2 * _nbytes((K, tn), w.dtype) + _nbytes((K, tn), BF16)
            + 2 * _nbytes((tm, tn), out_dtype) + _nbytes((tm, tn), F32) + (4 << 20))
    a_mode = {} if a_buffers == 2 else {"pipeline_mode": pl.Buffered(a_buffers)}
    w_spec = (pl.BlockSpec((None, tn, K), lambda i, j: (layer, j, 0)) if w_transposed
              else pl.BlockSpec((None, K, tn), lambda i, j: (layer, 0, j)))
    return pl.pallas_call(
        functools.partial(_mm_kernel, w_transposed=w_transposed),
        out_shape=jax.ShapeDtypeStruct((M, n_out), out_dtype),
        grid=(M // tm, n_out // tn),
        in_specs=[pl.BlockSpec((tm, K), lambda i, j: (i, 0), **a_mode), w_spec],
        out_specs=pl.BlockSpec((tm, tn), lambda i, j: (i, j)),
        compiler_params=_params(("parallel", "arbitrary"), vmem),
        name=name,
    )(a, w)


def _cast_kernel(x_ref, o_ref):
    o_ref[...] = x_ref[...].astype(o_ref.dtype)


def _cast_bf16(w):
    L, K, N = w.shape
    tk = _divisor(K, 512, 16)
    return pl.pallas_call(
        _cast_kernel,
        out_shape=jax.ShapeDtypeStruct((L, K, N), BF16),
        grid=(L, K // tk),
        in_specs=[pl.BlockSpec((None, tk, N), lambda l, k: (l, k, 0))],
        out_specs=pl.BlockSpec((None, tk, N), lambda l, k: (l, k, 0)),
        compiler_params=_params(("parallel", "arbitrary"), 48 << 20),
        name="cast_bf16",
    )(w)


def _swiglu_up_kernel(a_ref, wg_ref, wu_ref, o_ref):
    a = a_ref[...]
    g = jnp.dot(a, wg_ref[...].astype(BF16), preferred_element_type=F32)
    u = jnp.dot(a, wu_ref[...].astype(BF16), preferred_element_type=F32)
    o_ref[...] = (g * jax.nn.sigmoid(g) * u).astype(o_ref.dtype)


def _swiglu_up(a, w_gu, layer, *, tn=256, name="swiglu_up"):
    M, K = a.shape
    F = w_gu.shape[2] // 2
    tm = _divisor(M, MM_ROWS, 16)
    tn = _divisor(F, tn, LANES)
    nf = F // tn
    vmem = (2 * _nbytes((tm, K), BF16) + 4 * _nbytes((K, tn), F32) + 2 * _nbytes((K, tn), BF16)
            + 2 * _nbytes((tm, tn), BF16) + 3 * _nbytes((tm, tn), F32) + (4 << 20))
    return pl.pallas_call(
        _swiglu_up_kernel,
        out_shape=jax.ShapeDtypeStruct((M, F), BF16),
        grid=(M // tm, nf),
        in_specs=[pl.BlockSpec((tm, K), lambda i, j: (i, 0)),
                  pl.BlockSpec((None, K, tn), lambda i, j: (layer, 0, j)),
                  pl.BlockSpec((None, K, tn), lambda i, j: (layer, 0, j + nf))],
        out_specs=pl.BlockSpec((tm, tn), lambda i, j: (i, j)),
        compiler_params=_params(("parallel", "arbitrary"), vmem),
        name=name,
    )(a, w_gu, w_gu)


def _adaln_kernel(c_ref, w_ref, b_ref, o_ref):
    c = c_ref[...]
    a = (c * jax.nn.sigmoid(c)).astype(BF16)
    o_ref[...] = jnp.dot(a, w_ref[...].astype(BF16), preferred_element_type=F32) + b_ref[...]


def _adaln(cond, w, b, layer, *, tn=512):
    R, D = cond.shape
    N = w.shape[2]
    tn = _divisor(N, tn, LANES)
    vmem = 2 * _nbytes((D, tn), F32) + _nbytes((D, tn), BF16) + (4 << 20)
    out = pl.pallas_call(
        _adaln_kernel,
        out_shape=jax.ShapeDtypeStruct((R, N), F32),
        grid=(N // tn,),
        in_specs=[pl.BlockSpec((R, D), lambda j: (0, 0)),
                  pl.BlockSpec((None, D, tn), lambda j: (layer, 0, j)),
                  pl.BlockSpec((None, 1, tn), lambda j: (layer, 0, j))],
        out_specs=pl.BlockSpec((R, tn), lambda j: (0, j)),
        compiler_params=_params(("arbitrary",), vmem),
        name="adaln",
    )(cond, w, b.reshape(b.shape[0], 1, N))
    return out.reshape(R, 6, D)


def _stream_specs(n_ctx_tiles, tr, D):
    ctx = pl.BlockSpec((1, tr, D), lambda b, r: (b, jnp.minimum(r, n_ctx_tiles - 1), 0))
    lat = pl.BlockSpec((1, tr, D), lambda b, r: (b, jnp.maximum(r - n_ctx_tiles, 0), 0))
    return ctx, lat


def _modulate_kernel(c_ref, x_ref, m_ref, o_ref, *, n_ctx_tiles, shift, scale):
    m = m_ref[0]

    def emit(h_ref):
        o_ref[0] = (h_ref[0] * (1.0 + m[scale:scale + 1]) + m[shift:shift + 1]).astype(o_ref.dtype)

    pl.when(pl.program_id(1) < n_ctx_tiles)(lambda: emit(c_ref))
    pl.when(pl.program_id(1) >= n_ctx_tiles)(lambda: emit(x_ref))


def _modulate(ctx, x, mods, *, shift, scale):
    B, C, D = ctx.shape
    S = C + x.shape[1]
    tr = _divisor(math.gcd(S, C), ROW_TILE, 16)
    nct = C // tr
    return pl.pallas_call(
        functools.partial(_modulate_kernel, n_ctx_tiles=nct, shift=shift, scale=scale),
        out_shape=jax.ShapeDtypeStruct((B, S, D), BF16),
        grid=(B, S // tr),
        in_specs=[*_stream_specs(nct, tr, D),
                  pl.BlockSpec((1, 6, D), lambda b, r: (jnp.where(r < nct, B, b), 0, 0))],
        out_specs=pl.BlockSpec((1, tr, D), lambda b, r: (b, r, 0)),
        compiler_params=_params(("parallel", "arbitrary"), 0),
        name="modulate",
    )(ctx, x, mods)


def _layer_norm_rows(z, g, b):
    mu = jnp.mean(z, axis=-1, keepdims=True)
    zc = z - mu
    var = jnp.mean(zc * zc, axis=-1, keepdims=True)
    return zc * lax.rsqrt(var + LN_EPS) * g + b


def _resid_ln_kernel(*refs, n_ctx_tiles, split_h, gate, shift, scale):
    h_refs, (y_ref, m_ref, mn_ref, g_ref, b_ref, hn_ref, f_ref) = refs[:-7], refs[-7:]
    m = m_ref[0]
    mn = mn_ref[0]

    def emit(h_ref):
        hn = _layer_norm_rows(ALPHA * h_ref[0] + m[gate:gate + 1] * y_ref[0], g_ref[...], b_ref[...])
        hn_ref[0] = hn
        f_ref[0] = (hn * (1.0 + mn[scale:scale + 1]) + mn[shift:shift + 1]).astype(f_ref.dtype)

    if split_h:
        pl.when(pl.program_id(1) < n_ctx_tiles)(lambda: emit(h_refs[0]))
        pl.when(pl.program_id(1) >= n_ctx_tiles)(lambda: emit(h_refs[1]))
    else:
        emit(h_refs[0])


def _resid_ln(h, y, mods, mods_next, ln_g, ln_b, *, n_ctx_rows, h_row_offset, gate, shift, scale, f_dtype=BF16):
    B, R, D = y.shape
    tr = _divisor(math.gcd(R, n_ctx_rows, h_row_offset), ROW_TILE, 16)
    nct = n_ctx_rows // tr
    off = h_row_offset // tr
    cond = lambda b, r: (jnp.where(r < nct, B, b), 0, 0)
    row = pl.BlockSpec((1, tr, D), lambda b, r: (b, r, 0))
    vec = pl.BlockSpec((1, D), lambda b, r: (0, 0))
    split_h = isinstance(h, tuple)
    if split_h:
        assert off == 0 and nct > 0
        h_arrays, h_specs = list(h), list(_stream_specs(nct, tr, D))
    else:
        h_arrays, h_specs = [h], [pl.BlockSpec((1, tr, D), lambda b, r: (b, r + off, 0))]
    return pl.pallas_call(
        functools.partial(_resid_ln_kernel, n_ctx_tiles=nct, split_h=split_h, gate=gate, shift=shift, scale=scale),
        out_shape=(jax.ShapeDtypeStruct((B, R, D), F32), jax.ShapeDtypeStruct((B, R, D), f_dtype)),
        grid=(B, R // tr),
        in_specs=[*h_specs, row, pl.BlockSpec((1, 6, D), cond), pl.BlockSpec((1, 6, D), cond), vec, vec],
        out_specs=(row, row),
        compiler_params=_params(("parallel", "arbitrary"), 48 << 20),
        name="resid_ln",
    )(*h_arrays, y, mods, mods_next, ln_g.reshape(1, D), ln_b.reshape(1, D))


def _gates_kernel(w_ref, a_ref, b_ref, o_ref):
    g = lax.dot_general(w_ref[...].astype(BF16), a_ref[0], (((1,), (1,)), ((), ())),
                        preferred_element_type=F32)
    o_ref[0] = g + b_ref[...]


def _mlstm_gates(a, w_t, layer, gate_b, *, first_row):
    B, S, D = a.shape
    G = gate_b.shape[0]
    assert first_row % G == 0
    ts = _divisor(S, 1088, LANES)
    return pl.pallas_call(
        _gates_kernel,
        out_shape=jax.ShapeDtypeStruct((B, G, S), F32),
        grid=(B, S // ts),
        in_specs=[pl.BlockSpec((None, G, D), lambda b, s: (layer, first_row // G, 0)),
                  pl.BlockSpec((1, ts, D), lambda b, s: (b, s, 0)),
                  pl.BlockSpec((G, 1), lambda b, s: (0, 0))],
        out_specs=pl.BlockSpec((1, G, ts), lambda b, s: (b, 0, s)),
        compiler_params=_params(("parallel", "arbitrary"), 0),
        name="mlstm_gates",
    )(w_t, a, gate_b.reshape(G, 1))


def _log_sigmoid(x):
    return jnp.minimum(x, 0.0) - jnp.log1p(jnp.exp(-jnp.abs(x)))


def _mlstm_scan_kernel(g_ref, q_ref, k_ref, v_ref, h_ref, c_ref, n_ref, m_ref, *, heads, dk, dv, chunk):
    direction = pl.program_id(1)
    step = pl.program_id(2)
    L = chunk

    @pl.when(step == 0)
    def _():
        c_ref[...] = jnp.zeros_like(c_ref)
        n_ref[...] = jnp.zeros_like(n_ref)
        m_ref[...] = jnp.zeros_like(m_ref)

    row = lax.broadcasted_iota(jnp.int32, (L, L), 0)
    col = lax.broadcasted_iota(jnp.int32, (L, L), 1)
    seen = jnp.where(direction == 0, col - row, row - col) <= 0
    eye = row == col

    def to_col(r):
        return jnp.sum(jnp.where(eye, r, 0.0), axis=1, keepdims=True)

    def to_row(c):
        return jnp.sum(jnp.where(eye, c, 0.0), axis=0, keepdims=True)

    for h in range(heads):
        ig = g_ref[0, pl.ds(direction * 2 * heads + h, 1), :]
        lf = _log_sigmoid(g_ref[0, pl.ds(direction * 2 * heads + heads + h, 1), :])
        q = q_ref[0, :, h * dk:(h + 1) * dk] * (dk ** -0.5)
        k = k_ref[0, :, h * dk:(h + 1) * dk]
        v = v_ref[0, :, h * dv:(h + 1) * dv]
        m_prev = m_ref[h]

        b_col = jnp.sum(jnp.where(seen, lf, 0.0), axis=1, keepdims=True)
        b_row = to_row(b_col)
        d = jnp.where(seen, b_col - b_row + ig, MASKED)
        inter = b_col + m_prev
        m_t = jnp.maximum(inter, jnp.max(d, axis=1, keepdims=True))
        s = lax.dot_general(q, k, (((1,), (1,)), ((), ())), preferred_element_type=F32) * jnp.exp(d - m_t)
        g_in = jnp.exp(inter - m_t)
        num = (g_in * jnp.dot(q, c_ref[h].astype(BF16), preferred_element_type=F32)
               + jnp.dot(s.astype(BF16), v, preferred_element_type=F32))
        qf = q.astype(F32)
        den = g_in * jnp.sum(qf * n_ref[h], axis=1, keepdims=True) + jnp.sum(s, axis=1, keepdims=True)
        h_ref[0, 0, :, h * dv:(h + 1) * dv] = (num / jnp.maximum(jnp.abs(den), jnp.exp(-m_t))).astype(h_ref.dtype)

        b_last = jnp.sum(lf, axis=1, keepdims=True)
        w_row = b_last - b_row + ig
        m_new = jnp.maximum(b_last + m_prev, jnp.max(w_row, axis=1, keepdims=True))
        decay = jnp.exp(b_last + m_prev - m_new)
        kw = k.astype(F32) * to_col(jnp.exp(w_row - m_new))
        c_ref[h] = decay * c_ref[h] + jnp.dot(kw.T.astype(BF16), v, preferred_element_type=F32)
        n_ref[h] = decay * n_ref[h] + jnp.sum(kw, axis=0, keepdims=True)
        m_ref[h] = m_new


def _mlstm_scan(p, gates, *, n_ctx_rows, heads, qk, vw):
    B, S, _ = p.shape
    L = _divisor(math.gcd(S, n_ctx_rows), ML_CHUNK, LANES)
    nc, ncc = S // L, n_ctx_rows // L
    dk, dv = qk // heads, vw // heads

    def chunk(d, j):
        back = jnp.where(j < ncc, ncc - 1 - j, nc - 1 - (j - ncc))
        return jnp.where(d == 0, j, back)

    vmem = (2 * (2 * _nbytes((L, qk), BF16) + _nbytes((L, vw), BF16) + _nbytes((L, vw), F32))
            + _nbytes((heads, dk, dv), F32) + (16 << 20))
    return pl.pallas_call(
        functools.partial(_mlstm_scan_kernel, heads=heads, dk=dk, dv=dv, chunk=L),
        out_shape=jax.ShapeDtypeStruct((2, B, S, vw), BF16),
        grid=(B, 2, nc),
        in_specs=[pl.BlockSpec((1, 4 * heads, L), lambda b, d, j: (b, 0, chunk(d, j))),
                  pl.BlockSpec((1, L, qk), lambda b, d, j: (b, chunk(d, j), 0)),
                  pl.BlockSpec((1, L, qk), lambda b, d, j: (b, chunk(d, j), 1)),
                  pl.BlockSpec((1, L, vw), lambda b, d, j: (b, chunk(d, j), (2 * qk) // vw))],
        out_specs=pl.BlockSpec((1, 1, L, vw), lambda b, d, j: (d, b, chunk(d, j), 0)),
        scratch_shapes=[pltpu.VMEM((heads, dk, dv), F32), pltpu.VMEM((heads, 1, dk), F32),
                        pltpu.VMEM((heads, 1, 1), F32)],
        compiler_params=_params(("parallel", "parallel", "arbitrary"), vmem),
        name="mlstm_scan",
    )(gates, p, p, p)


def _mlstm_norm_kernel(h_ref, o_ref, g_ref, out_ref, *, heads, dv):
    hs = h_ref[0, 0].astype(F32) + h_ref[1, 0].astype(F32)
    for h in range(heads):
        x = hs[:, h * dv:(h + 1) * dv]
        x = x * lax.rsqrt(jnp.mean(x * x, axis=-1, keepdims=True) + RMS_EPS)
        x = x * g_ref[:, h * dv:(h + 1) * dv] * jax.nn.sigmoid(o_ref[0, :, h * dv:(h + 1) * dv].astype(F32))
        out_ref[0, :, h * dv:(h + 1) * dv] = x.astype(out_ref.dtype)


def _mlstm_norm(hfb, p, norm_g, *, heads, o_col_block):
    _, B, S, vw = hfb.shape
    tr = _divisor(S, ROW_TILE, 16)
    return pl.pallas_call(
        functools.partial(_mlstm_norm_kernel, heads=heads, dv=vw // heads),
        out_shape=jax.ShapeDtypeStruct((B, S, vw), BF16),
        grid=(B, S // tr),
        in_specs=[pl.BlockSpec((2, 1, tr, vw), lambda b, r: (0, b, r, 0)),
                  pl.BlockSpec((1, tr, vw), lambda b, r: (b, r, o_col_block)),
                  pl.BlockSpec((1, vw), lambda b, r: (0, 0))],
        out_specs=pl.BlockSpec((1, tr, vw), lambda b, r: (b, r, 0)),
        compiler_params=_params(("parallel", "arbitrary"), 48 << 20),
        name="mlstm_norm",
    )(hfb, p, norm_g.reshape(1, vw))


def _qkv_rope_kernel(a_ref, w_ref, cos_ref, sin_ref, o_ref, *, rope_blocks):
    j = pl.program_id(1)
    acc = jnp.dot(a_ref[...], w_ref[...].astype(BF16), preferred_element_type=F32)

    @pl.when(j < rope_blocks)
    def _():
        tn = acc.shape[1]
        lane = lax.broadcasted_iota(jnp.int32, (1, LANES), 1)
        first = (lane // (LANES // 4)) % 2 == 0
        for hd in range(tn // LANES):
            x = acc[:, hd * LANES:(hd + 1) * LANES]
            partner = jnp.where(first, pltpu.roll(x, LANES - LANES // 4, 1), pltpu.roll(x, LANES // 4, 1))
            o_ref[:, hd * LANES:(hd + 1) * LANES] = (x * cos_ref[...] + partner * sin_ref[...]).astype(o_ref.dtype)

    @pl.when(j >= rope_blocks)
    def _():
        o_ref[...] = acc.astype(o_ref.dtype)


def _qkv_rope(a, w, layer, cos, sin, *, rope_cols, rows_per_batch, tn=512):
    M, K = a.shape
    N = w.shape[2]
    tm = _divisor(rows_per_batch, MM_ROWS, 16)
    tn = _divisor(math.gcd(N, rope_cols), tn, LANES)
    ntab = rows_per_batch // tm
    vmem = (2 * _nbytes((tm, K), BF16) + 2 * _nbytes((K, tn), F32) + _nbytes((K, tn), BF16)
            + 2 * _nbytes((tm, tn), BF16) + 3 * _nbytes((tm, tn), F32) + (6 << 20))
    return pl.pallas_call(
        functools.partial(_qkv_rope_kernel, rope_blocks=rope_cols // tn),
        out_shape=jax.ShapeDtypeStruct((M, N), BF16),
        grid=(M // tm, N // tn),
        in_specs=[pl.BlockSpec((tm, K), lambda i, j: (i, 0)),
                  pl.BlockSpec((None, K, tn), lambda i, j: (layer, 0, j)),
                  pl.BlockSpec((tm, LANES), lambda i, j: (i % ntab, 0)),
                  pl.BlockSpec((tm, LANES), lambda i, j: (i % ntab, 0))],
        out_specs=pl.BlockSpec((tm, tn), lambda i, j: (i, j)),
        compiler_params=_params(("parallel", "arbitrary"), vmem),
        name="qkv_rope",
    )(a, w, cos, sin)


def _attention_kernel(sink_ref, q_ref, kp_ref, kc_ref, kn_ref, kx_ref, vp_ref, vc_ref, vn_ref, vx_ref, o_ref,
                      *, group, kv_step, n_blocks, scale):
    n = pl.program_id(2)
    T = q_ref.shape[1]
    row = lax.broadcasted_iota(jnp.int32, (group * T, T), 0) % T
    col = lax.broadcasted_iota(jnp.int32, (group * T, T), 1)
    prev_ok = (col >= row) & (n > 0)
    next_ok = (col <= row) & (n < n_blocks - 1)
    for h in range(kv_step):
        kv = pl.program_id(1) * kv_step + h
        hs = slice(h * LANES, (h + 1) * LANES)
        q = jnp.concatenate([q_ref[0, :, (h * group + g) * LANES:(h * group + g + 1) * LANES] for g in range(group)],
                            axis=0)
        k = jnp.concatenate([kp_ref[0, :, hs], kc_ref[0, :, hs], kn_ref[0, :, hs], kx_ref[0, :, hs]], axis=0)
        v = jnp.concatenate([vp_ref[0, :, hs], vc_ref[0, :, hs], vn_ref[0, :, hs], vx_ref[0, :, hs]], axis=0)
        s = lax.dot_general(q, k, (((1,), (1,)), ((), ())), preferred_element_type=F32) * scale
        s_p = jnp.where(prev_ok, s[:, :T], MASKED)
        s_c = s[:, T:2 * T]
        s_n = jnp.where(next_ok, s[:, 2 * T:3 * T], MASKED)
        s_x = s[:, 3 * T:]
        sink = jnp.concatenate([jnp.full((T, 1), sink_ref[kv * group + g], F32) for g in range(group)], axis=0)
        m = jnp.maximum(jnp.maximum(jnp.max(s_p, axis=1, keepdims=True), jnp.max(s_c, axis=1, keepdims=True)),
                        jnp.maximum(jnp.max(s_n, axis=1, keepdims=True), jnp.max(s_x, axis=1, keepdims=True)))
        m = jnp.maximum(m, sink)
        e_p, e_c, e_n, e_x = jnp.exp(s_p - m), jnp.exp(s_c - m), jnp.exp(s_n - m), jnp.exp(s_x - m)
        total = (jnp.sum(e_p, axis=1, keepdims=True) + jnp.sum(e_c, axis=1, keepdims=True)
                 + jnp.sum(e_n, axis=1, keepdims=True) + jnp.sum(e_x, axis=1, keepdims=True) + jnp.exp(sink - m))
        inv = 1.0 / total
        p = jnp.concatenate([(e_p * inv).astype(BF16), (e_c * inv).astype(BF16), (e_n * inv).astype(BF16),
                             (e_x * inv).astype(BF16)], axis=1)
        o = jnp.dot(p, v, preferred_element_type=F32)
        for g in range(group):
            o_ref[0, :, (h * group + g) * LANES:(h * group + g + 1) * LANES] = o[g * T:(g + 1) * T].astype(o_ref.dtype)


def _window_attention(p, sink, *, n_ctx_rows, heads, kv_heads):
    B, S, _ = p.shape
    T = S - n_ctx_rows
    assert WINDOW == Q_BLOCK and T % Q_BLOCK == 0 and n_ctx_rows % Q_BLOCK == 0
    nb = T // Q_BLOCK
    cb = n_ctx_rows // Q_BLOCK
    group = heads // kv_heads
    kv_step = 4 if kv_heads % 4 == 0 else 1
    kw = kv_step * LANES
    kcol, vcol = heads // kv_step, (heads + kv_heads) // kv_step

    def blk(colbase, shift):
        def index(b, kv, n):
            return (b, jnp.clip(n + shift, 0, nb - 1) + cb, colbase + kv)
        return pl.BlockSpec((1, Q_BLOCK, kw), index)

    def ctx(colbase):
        return pl.BlockSpec((1, n_ctx_rows, kw), lambda b, kv, n: (b, 0, colbase + kv))

    qw = kv_step * group * LANES
    return pl.pallas_call(
        functools.partial(_attention_kernel, group=group, kv_step=kv_step, n_blocks=nb, scale=LANES ** -0.5),
        out_shape=jax.ShapeDtypeStruct((B, T, heads * LANES), BF16),
        grid=(B, kv_heads // kv_step, nb),
        in_specs=[pl.BlockSpec(memory_space=pltpu.SMEM),
                  pl.BlockSpec((1, Q_BLOCK, qw), lambda b, kv, n: (b, n + cb, kv)),
                  blk(kcol, -1), blk(kcol, 0), blk(kcol, 1), ctx(kcol),
                  blk(vcol, -1), blk(vcol, 0), blk(vcol, 1), ctx(vcol)],
        out_specs=pl.BlockSpec((1, Q_BLOCK, qw), lambda b, kv, n: (b, n, kv)),
        compiler_params=_params(("parallel", "parallel", "arbitrary"), 0),
        name="window_attention",
    )(sink, p, p, p, p, p, p, p, p, p)


def _rope_tables(n_ctx_rows, seq):
    quarter = LANES // 4
    freqs = ROPE_THETA ** (-jnp.arange(quarter, dtype=F32) / quarter)
    t = jnp.arange(seq)
    ang_r = (t // GRID_W).astype(F32)[:, None] * freqs[None, :]
    ang_c = (t % GRID_W).astype(F32)[:, None] * freqs[None, :]
    cos = jnp.concatenate([jnp.cos(ang_r), jnp.cos(ang_r), jnp.cos(ang_c), jnp.cos(ang_c)], axis=-1)
    sin = jnp.concatenate([-jnp.sin(ang_r), jnp.sin(ang_r), -jnp.sin(ang_c), jnp.sin(ang_c)], axis=-1)
    cos = jnp.concatenate([jnp.ones((n_ctx_rows, LANES), F32), cos], axis=0)
    sin = jnp.concatenate([jnp.zeros((n_ctx_rows, LANES), F32), sin], axis=0)
    return cos, sin


def _router_kernel(w_ref, f_ref, idx_ref, wt_ref):
    logits = lax.dot_general(w_ref[...].astype(BF16), f_ref[...].astype(BF16), (((1,), (1,)), ((), ())),
                             preferred_element_type=F32)
    n_exp = logits.shape[0]
    eid = lax.broadcasted_iota(jnp.int32, logits.shape, 0)
    v1 = jnp.max(logits, axis=0, keepdims=True)
    i1 = jnp.min(jnp.where(logits == v1, eid, n_exp), axis=0, keepdims=True)
    rest = jnp.where(eid == i1, -jnp.inf, logits)
    v2 = jnp.max(rest, axis=0, keepdims=True)
    i2 = jnp.min(jnp.where(rest == v2, eid, n_exp), axis=0, keepdims=True)
    e2 = jnp.exp(v2 - v1)
    idx_ref[0:1, :] = i1
    idx_ref[1:2, :] = i2
    wt_ref[0:1, :] = 1.0 / (1.0 + e2)
    wt_ref[1:2, :] = e2 / (1.0 + e2)


def _router(f, w_router_t):
    T, D = f.shape
    E = w_router_t.shape[0]
    tt = _divisor(T, 1024, LANES)
    return pl.pallas_call(
        _router_kernel,
        out_shape=(jax.ShapeDtypeStruct((TOP_K, T), jnp.int32), jax.ShapeDtypeStruct((TOP_K, T), F32)),
        grid=(T // tt,),
        in_specs=[pl.BlockSpec((E, D), lambda i: (0, 0)), pl.BlockSpec((tt, D), lambda i: (i, 0))],
        out_specs=(pl.BlockSpec((TOP_K, tt), lambda i: (0, i)), pl.BlockSpec((TOP_K, tt), lambda i: (0, i))),
        compiler_params=_params(("arbitrary",), 48 << 20),
        name="moe_router",
    )(w_router_t, f)


GATHER_ROWS = MOE_ROWS // MOE_PARTS
GATHER_UNROLL = 8


def _gather_kernel(idx_ref, tp_ref, src_ref, o_ref, stage, sems):
    i = pl.program_id(0)
    last = pl.num_programs(0) - 1

    def used(step):
        return (step % MOE_PARTS) < tp_ref[step // MOE_PARTS]

    def issue(step, slot):
        def body(r, carry):
            row = idx_ref[step * GATHER_ROWS + r]
            pltpu.make_async_copy(src_ref.at[pl.ds(row, 1), :], stage.at[slot, pl.ds(r, 1), :], sems.at[slot]).start()
            return carry

        lax.fori_loop(0, GATHER_ROWS, body, 0, unroll=GATHER_UNROLL)

    @pl.when(jnp.logical_and(i == 0, used(0)))
    def _():
        issue(0, 0)

    nxt = jnp.minimum(i + 1, last)

    @pl.when(jnp.logical_and(i < last, used(nxt)))
    def _():
        issue(nxt, nxt % 2)

    @pl.when(used(i))
    def _():
        slot = i % 2
        pltpu.make_async_copy(src_ref.at[pl.ds(0, GATHER_ROWS), :], stage.at[slot], sems.at[slot]).wait()
        o_ref[...] = stage[slot].astype(o_ref.dtype)

    @pl.when(jnp.logical_not(used(i)))
    def _():
        o_ref[...] = jnp.zeros_like(o_ref)


def _gather_rows(src, idx, tile_parts, out_dtype):
    N, D = src.shape
    R = idx.shape[0]
    assert R % MOE_ROWS == 0 and D % LANES == 0
    return pl.pallas_call(
        _gather_kernel,
        out_shape=jax.ShapeDtypeStruct((R, D), out_dtype),
        grid_spec=pltpu.PrefetchScalarGridSpec(
            num_scalar_prefetch=2,
            grid=(R // GATHER_ROWS,),
            in_specs=[pl.BlockSpec(memory_space=pl.ANY)],
            out_specs=pl.BlockSpec((GATHER_ROWS, D), lambda i, idx_ref, tp_ref: (i, 0)),
            scratch_shapes=[pltpu.VMEM((2, GATHER_ROWS, D), src.dtype), pltpu.SemaphoreType.DMA((2,))],
        ),
        compiler_params=_params(("arbitrary",), 0),
        name="gather_rows",
    )(idx, tile_parts, src)


def _for_used_parts(parts, o_ref, emit):
    part = o_ref.shape[0] // MOE_PARTS
    for n in range(MOE_PARTS + 1):
        @pl.when(parts == n)
        def _(n=n):
            if n > 0:
                emit(slice(0, n * part))
            if n < MOE_PARTS:
                o_ref[n * part:, :] = jnp.zeros(((MOE_PARTS - n) * part, o_ref.shape[1]), o_ref.dtype)


def _expert_up_kernel(te_ref, tp_ref, ts_ref, x_ref, wg_ref, wu_ref, o_ref, wgb_ref, wub_ref):
    i = pl.program_id(1)
    fresh = jnp.logical_or(i == 0, te_ref[i] != te_ref[jnp.maximum(i - 1, 0)])

    @pl.when(fresh)
    def _():
        wgb_ref[...] = wg_ref[...].astype(BF16)
        wub_ref[...] = wu_ref[...].astype(BF16)

    def emit(rows):
        x = x_ref[rows, :]
        g = jnp.dot(x, wgb_ref[...], preferred_element_type=F32)
        u = jnp.dot(x, wub_ref[...], preferred_element_type=F32)
        o_ref[rows, :] = (g * jax.nn.sigmoid(g) * u).astype(o_ref.dtype)

    _for_used_parts(tp_ref[i], o_ref, emit)


def _expert_up(xs, w_gu, layer, tiles, *, tn=256):
    P, D = xs.shape
    F2 = w_gu.shape[3]
    F = F2 // 2
    tm = MOE_ROWS
    tn = _divisor(F, tn, LANES)
    nf = F // tn
    vmem = (2 * _nbytes((tm, D), BF16) + 4 * _nbytes((D, tn), F32) + 2 * _nbytes((D, tn), BF16)
            + 2 * _nbytes((tm, tn), BF16) + 3 * _nbytes((tm, tn), F32) + (4 << 20))
    return pl.pallas_call(
        _expert_up_kernel,
        out_shape=jax.ShapeDtypeStruct((P, F), BF16),
        grid_spec=pltpu.PrefetchScalarGridSpec(
            num_scalar_prefetch=3,
            grid=(nf, P // tm),
            in_specs=[pl.BlockSpec((tm, D), lambda j, i, te, th, ts: (ts[i], 0)),
                      pl.BlockSpec((None, None, D, tn), lambda j, i, te, th, ts: (layer, te[i], 0, j)),
                      pl.BlockSpec((None, None, D, tn), lambda j, i, te, th, ts: (layer, te[i], 0, j + nf))],
            out_specs=pl.BlockSpec((tm, tn), lambda j, i, te, th, ts: (i, j)),
            scratch_shapes=[pltpu.VMEM((D, tn), BF16), pltpu.VMEM((D, tn), BF16)],
        ),
        compiler_params=_params(("arbitrary", "arbitrary"), vmem),
        name="expert_up",
    )(*tiles, xs, w_gu, w_gu)


def _expert_down_kernel(te_ref, tp_ref, ts_ref, a_ref, w_ref, o_ref, wb_ref):
    i = pl.program_id(1)
    fresh = jnp.logical_or(i == 0, te_ref[i] != te_ref[jnp.maximum(i - 1, 0)])

    @pl.when(fresh)
    def _():
        wb_ref[...] = w_ref[...].astype(BF16)

    def emit(rows):
        o_ref[rows, :] = jnp.dot(a_ref[rows, :], wb_ref[...], preferred_element_type=F32)

    _for_used_parts(tp_ref[i], o_ref, emit)


def _expert_down(act, w_down, layer, tiles, *, tn=512):
    P, F = act.shape
    D = w_down.shape[3]
    tm = MOE_ROWS
    tn = _divisor(D, tn, LANES)
    vmem = (2 * _nbytes((tm, F), BF16) + 2 * _nbytes((F, tn), F32) + _nbytes((F, tn), BF16)
            + 3 * _nbytes((tm, tn), F32) + (4 << 20))
    return pl.pallas_call(
        _expert_down_kernel,
        out_shape=jax.ShapeDtypeStruct((P, D), F32),
        grid_spec=pltpu.PrefetchScalarGridSpec(
            num_scalar_prefetch=3,
            grid=(D // tn, P // tm),
            in_specs=[pl.BlockSpec((tm, F), lambda j, i, te, th, ts: (ts[i], 0)),
                      pl.BlockSpec((None, None, F, tn), lambda j, i, te, th, ts: (layer, te[i], 0, j))],
            out_specs=pl.BlockSpec((tm, tn), lambda j, i, te, th, ts: (i, j)),
            scratch_shapes=[pltpu.VMEM((F, tn), BF16)],
        ),
        compiler_params=_params(("arbitrary", "arbitrary"), vmem),
        name="expert_down",
    )(*tiles, act, w_down)


def _routing_tables(idx, n_experts):
    K, T = idx.shape
    tm = MOE_ROWS
    n_tiles = (K * T) // tm + n_experts
    P = n_tiles * tm
    e = idx.reshape(-1)
    onehot = (e[:, None] == jnp.arange(n_experts)[None, :]).astype(jnp.int32)
    csum = jnp.cumsum(onehot, axis=0)
    rank = jnp.sum(csum * onehot, axis=1) - 1
    counts = csum[-1]
    tiles_per = (counts + tm - 1) // tm
    tile_end = jnp.cumsum(tiles_per)
    start = (tile_end - tiles_per) * tm
    slot = jnp.sum(onehot * start[None, :], axis=1) + rank
    token = jnp.tile(jnp.arange(T, dtype=jnp.int32), K)
    slot_token = jnp.zeros((P,), jnp.int32).at[slot].set(token)
    tiles = jnp.arange(n_tiles, dtype=jnp.int32)
    owner = jnp.sum((tile_end[None, :] <= tiles[:, None]).astype(jnp.int32), axis=1)
    owns = (owner[:, None] == jnp.arange(n_experts)[None, :]).astype(jnp.int32)
    rows_used = jnp.clip(jnp.sum(owns * (start + counts)[None, :], axis=1) - tiles * tm, 0, tm)
    part = tm // MOE_PARTS
    tile_parts = ((rows_used + part - 1) // part).astype(jnp.int32)
    last_owner = jnp.sum((tile_end <= tile_end[-1] - 1).astype(jnp.int32))
    tile_expert = jnp.where(rows_used > 0, owner, last_owner).astype(jnp.int32)
    tile_src = jnp.minimum(tiles, tile_end[-1] - 1).astype(jnp.int32)
    return slot_token, slot.reshape(K, T).astype(jnp.int32), (tile_expert, tile_parts, tile_src)


def _combine_ln_kernel(slot_ref, ys_ref, h_ref, w_ref, m_ref, g_ref, b_ref, o_ref, ybuf, sems, *, gate, n_tokens):
    s = pl.program_id(0)
    tr = o_ref.shape[0]

    def issue(step, buf):
        def body(r, carry):
            for k in range(TOP_K):
                row = slot_ref[k * n_tokens + step * tr + r]
                pltpu.make_async_copy(ys_ref.at[pl.ds(row, 1), :], ybuf.at[buf, k, pl.ds(r, 1), :], sems.at[buf]).start()
            return carry
        lax.fori_loop(0, tr, body, 0, unroll=GATHER_UNROLL // TOP_K)

    @pl.when(s == 0)
    def _():
        issue(0, 0)

    @pl.when(s + 1 < pl.num_programs(0))
    def _():
        issue(s + 1, (s + 1) % 2)

    buf = s % 2
    for k in range(TOP_K):
        pltpu.make_async_copy(ys_ref.at[pl.ds(0, tr), :], ybuf.at[buf, k], sems.at[buf]).wait()
    y = w_ref[:, 0:1] * ybuf[buf, 0]
    for k in range(1, TOP_K):
        y = y + w_ref[:, k:k + 1] * ybuf[buf, k]
    m = m_ref[0]
    o_ref[...] = _layer_norm_rows(ALPHA * h_ref[...] + m[gate:gate + 1] * y, g_ref[...], b_ref[...])


def _combine_ln(h, ys, pair_slot, wts, mods, ln_g, ln_b, *, gate):
    B, T, D = h.shape
    n_tokens = B * T
    tr = _divisor(T, ROW_TILE, 16)
    row = pl.BlockSpec((tr, D), lambda i, slots: (i, 0))
    vec = pl.BlockSpec((1, D), lambda i, slots: (0, 0))
    out = pl.pallas_call(
        functools.partial(_combine_ln_kernel, gate=gate, n_tokens=n_tokens),
        out_shape=jax.ShapeDtypeStruct((n_tokens, D), F32),
        grid_spec=pltpu.PrefetchScalarGridSpec(
            num_scalar_prefetch=1,
            grid=(n_tokens // tr,),
            in_specs=[pl.BlockSpec(memory_space=pl.ANY), row,
                      pl.BlockSpec((tr, TOP_K), lambda i, slots: (i, 0)),
                      pl.BlockSpec((1, 6, D), lambda i, slots: (i // (T // tr), 0, 0)), vec, vec],
            out_specs=row,
            scratch_shapes=[pltpu.VMEM((2, TOP_K, tr, D), F32), pltpu.SemaphoreType.DMA((2,))],
        ),
        compiler_params=_params(("arbitrary",), 2 * TOP_K * _nbytes((tr, D), F32) + 5 * _nbytes((tr, D), F32) + (4 << 20)),
        name="combine_ln",
    )(pair_slot.reshape(-1), ys, h.reshape(n_tokens, D), wts.T, mods, ln_g.reshape(1, D), ln_b.reshape(1, D))
    return out.reshape(B, T, D)


SHIFT_MIX, SCALE_MIX, GATE_MIX, SHIFT_FFN, SCALE_FFN, GATE_FFN = range(6)


def kernel(x, c, ctx, c_ctx, ada_w, ada_b, ln_g, ln_b, ml_w_in, ml_gate_b, ml_norm_g, ml_w_out, att_w_in, att_sink,
           att_w_out, ffn_w_gu, ffn_w_down, moe_router, moe_w_gu, moe_w_down):
    B, T, D = x.shape
    C = ctx.shape[1]
    S = C + T
    assert ada_w.shape[0] == DEPTH == 2
    qk, vw = ml_w_out.shape[1] // 2, ml_w_out.shape[1]
    n_experts = moe_router.shape[-1]

    cond = jnp.concatenate([c, c_ctx[None, :]], axis=0)
    mods = [_adaln(cond, ada_w, ada_b, i) for i in range(DEPTH)]

    a = _modulate(ctx, x, mods[0], shift=SHIFT_MIX, scale=SCALE_MIX)
    n_main = 2 * qk + 2 * vw
    w_in_t = jnp.swapaxes(ml_w_in, 1, 2)
    p = _matmul(a.reshape(B * S, D), w_in_t, 0, n_out=n_main, out_dtype=BF16, w_transposed=True,
                name="mlstm_in").reshape(B, S, n_main)
    gates = _mlstm_gates(a, w_in_t, 0, ml_gate_b[0], first_row=n_main)
    hfb = _mlstm_scan(p, gates, n_ctx_rows=C, heads=ML_HEADS, qk=qk, vw=vw)
    hn = _mlstm_norm(hfb, p, ml_norm_g[0], heads=ML_HEADS, o_col_block=(2 * qk + vw) // vw)
    y = _matmul(hn.reshape(B * S, vw), ml_w_out, 0, out_dtype=BF16, name="mlstm_out").reshape(B, S, D)
    h1, f = _resid_ln((ctx, x), y, mods[0], mods[0], ln_g[0, 0], ln_b[0, 0], n_ctx_rows=C, h_row_offset=0,
                      gate=GATE_MIX, shift=SHIFT_FFN, scale=SCALE_FFN)
    act = _swiglu_up(f.reshape(B * S, D), ffn_w_gu, 0)
    y = _matmul(act, _cast_bf16(ffn_w_down), 0, out_dtype=BF16, tn=256, a_buffers=1,
                name="ffn_down").reshape(B, S, D)
    h2, a = _resid_ln(h1, y, mods[0], mods[1], ln_g[0, 1], ln_b[0, 1], n_ctx_rows=C, h_row_offset=0,
                      gate=GATE_FFN, shift=SHIFT_MIX, scale=SCALE_MIX)

    q_cols = ATT_HEADS * LANES
    cos, sin = _rope_tables(C, T)
    p = _qkv_rope(a.reshape(B * S, D), att_w_in, 0, cos, sin, rope_cols=q_cols + ATT_KV * LANES,
                  rows_per_batch=S).reshape(B, S, -1)
    o = _window_attention(p, att_sink[0], n_ctx_rows=C, heads=ATT_HEADS, kv_heads=ATT_KV)
    y = _matmul(o.reshape(B * T, q_cols), att_w_out, 0, out_dtype=BF16, name="att_out").reshape(B, T, D)
    h3, f = _resid_ln(h2, y, mods[1], mods[1], ln_g[1, 0], ln_b[1, 0], n_ctx_rows=0, h_row_offset=C,
                      gate=GATE_MIX, shift=SHIFT_FFN, scale=SCALE_FFN, f_dtype=F32)

    f2 = f.reshape(B * T, D)
    idx, wts = _router(f2, moe_router[0].T)
    slot_token, pair_slot, tiles = _routing_tables(idx, n_experts)
    xs = _gather_rows(f2, slot_token, tiles[1], BF16)
    act = _expert_up(xs, moe_w_gu, 0, tiles)
    ys = _expert_down(act, moe_w_down, 0, tiles)
    return _combine_ln(h3, ys, pair_slot, wts, mods[1], ln_g[1, 1], ln_b[1, 1], gate=GATE_FFN)
```

```python
import functools
import math

import jax
import jax.numpy as jnp
from jax import lax
from jax.experimental import pallas as pl
from jax.experimental.pallas import tpu as pltpu

F32 = jnp.float32
BF16 = jnp.bfloat16

GRID_W = 64
ML_HEADS = 8
ATT_HEADS = 32
ATT_KV = 8
WINDOW = 128
Q_BLOCK = 128
ROPE_THETA = 10000.0
TOP_K = 2
LN_EPS = 1e-5
RMS_EPS = 1e-6
DEPTH = 2
ALPHA = (2 * DEPTH) ** 0.25

V7X_VMEM_BYTES = 64 * 1024 * 1024
VMEM_CAP = V7X_VMEM_BYTES - 8 * 1024 * 1024
LANES = 128
MASKED = -1e30

ROW_TILE = 256
MM_ROWS = 1088
ML_CHUNK = 256
MOE_ROWS = 1024
MOE_PARTS = 4


def _divisor(n, target, mult):
    best = None
    for d in range(mult, min(n, target) + 1, mult):
        if n % d == 0:
            best = d
    return best if best is not None else n


def _params(sem, vmem_bytes):
    limit = int(min(VMEM_CAP, max(32 * 1024 * 1024, vmem_bytes)))
    return pltpu.CompilerParams(dimension_semantics=sem, vmem_limit_bytes=limit)


def _nbytes(shape, dtype):
    return math.prod(shape) * jnp.dtype(dtype).itemsize


def _mm_kernel(a_ref, w_ref, o_ref, *, w_transposed):
    contract = (((1,), (1 if w_transposed else 0,)), ((), ()))
    o_ref[...] = lax.dot_general(a_ref[...], w_ref[...].astype(BF16), contract,
                                 preferred_element_type=F32).astype(o_ref.dtype)


def _matmul(a, w, layer, *, n_out=None, out_dtype=F32, tn=512, a_buffers=2, w_transposed=False, name="matmul"):
    M, K = a.shape
    n_out = w.shape[1 if w_transposed else 2] if n_out is None else n_out
    tm = _divisor(M, MM_ROWS, 16)
    tn = _divisor(n_out, tn, LANES)
    vmem = (a_buffers * _nbytes((tm, K), BF16) + 2 * _nbytes((K, tn), w.dtype) + _nbytes((K, tn), BF16)
            + 2 * _nbytes((tm, tn), out_dtype) + _nbytes((tm, tn), F32) + (4 << 20))
    a_mode = {} if a_buffers == 2 else {"pipeline_mode": pl.Buffered(a_buffers)}
    w_spec = (pl.BlockSpec((None, tn, K), lambda i, j: (layer, j, 0)) if w_transposed
              else pl.BlockSpec((None, K, tn), lambda i, j: (layer, 0, j)))
    return pl.pallas_call(
        functools.partial(_mm_kernel, w_transposed=w_transposed),
        out_shape=jax.ShapeDtypeStruct((M, n_out), out_dtype),
        grid=(M // tm, n_out // tn),
        in_specs=[pl.BlockSpec((tm, K), lambda i, j: (i, 0), **a_mode), w_spec],
        out_specs=pl.BlockSpec((tm, tn), lambda i, j: (i, j)),
        compiler_params=_params(("parallel", "arbitrary"), vmem),
        name=name,
    )(a, w)


def _cast_kernel(x_ref, o_ref):
    o_ref[...] = x_ref[...].astype(o_ref.dtype)


def _cast_bf16(w):
    L, K, N = w.shape
    tk = _divisor(K, 512, 16)
    return pl.pallas_call(
        _cast_kernel,
        out_shape=jax.ShapeDtypeStruct((L, K, N), BF16),
        grid=(L, K // tk),
        in_specs=[pl.BlockSpec((None, tk, N), lambda l, k: (l, k, 0))],
        out_specs=pl.BlockSpec((None, tk, N), lambda l, k: (l, k, 0)),
        compiler_params=_params(("parallel", "arbitrary"), 48 << 20),
        name="cast_bf16",
    )(w)


def _swiglu_up_kernel(a_ref, wg_ref, wu_ref, c_ref, aw_ref, ab_ref, o_ref, m_ref, *, nf, side_blocks):
    a = a_ref[...]
    g = jnp.dot(a, wg_ref[...].astype(BF16), preferred_element_type=F32)
    u = jnp.dot(a, wu_ref[...].astype(BF16), preferred_element_type=F32)
    o_ref[...] = (g * jax.nn.sigmoid(g) * u).astype(o_ref.dtype)

    @pl.when(pl.program_id(0) * nf + pl.program_id(1) < side_blocks)
    def _():
        c = c_ref[...]
        s = (c * jax.nn.sigmoid(c)).astype(BF16)
        m_ref[...] = jnp.dot(s, aw_ref[...].astype(BF16), preferred_element_type=F32) + ab_ref[...]


def _swiglu_up(a, w_gu, layer, cond, ada_w, ada_b, ada_layer, *, tn=256, name="swiglu_up"):
    M, K = a.shape
    F = w_gu.shape[2] // 2
    tm = _divisor(M, MM_ROWS, 16)
    tn = _divisor(F, tn, LANES)
    nf = F // tn
    R, D = cond.shape
    N = ada_w.shape[2]
    side_blocks = N // LANES
    assert N % LANES == 0 and side_blocks <= (M // tm) * nf
    side = lambda i, j: jnp.minimum(i * nf + j, side_blocks - 1)
    vmem = (2 * _nbytes((tm, K), BF16) + 4 * _nbytes((K, tn), F32) + 2 * _nbytes((K, tn), BF16)
            + 2 * _nbytes((tm, tn), BF16) + 3 * _nbytes((tm, tn), F32)
            + 2 * _nbytes((D, LANES), F32) + _nbytes((D, LANES), BF16) + (4 << 20))
    act, mods = pl.pallas_call(
        functools.partial(_swiglu_up_kernel, nf=nf, side_blocks=side_blocks),
        out_shape=(jax.ShapeDtypeStruct((M, F), BF16), jax.ShapeDtypeStruct((R, N), F32)),
        grid=(M // tm, nf),
        in_specs=[pl.BlockSpec((tm, K), lambda i, j: (i, 0)),
                  pl.BlockSpec((None, K, tn), lambda i, j: (layer, 0, j)),
                  pl.BlockSpec((None, K, tn), lambda i, j: (layer, 0, j + nf)),
                  pl.BlockSpec((R, D), lambda i, j: (0, 0)),
                  pl.BlockSpec((None, D, LANES), lambda i, j: (ada_layer, 0, side(i, j))),
                  pl.BlockSpec((None, 1, LANES), lambda i, j: (ada_layer, 0, side(i, j)))],
        out_specs=(pl.BlockSpec((tm, tn), lambda i, j: (i, j)),
                   pl.BlockSpec((R, LANES), lambda i, j: (0, side(i, j)))),
        compiler_params=_params(("arbitrary", "arbitrary"), vmem),
        name=name,
    )(a, w_gu, w_gu, cond, ada_w, ada_b.reshape(ada_b.shape[0], 1, N))
    return act, mods.reshape(R, 6, D)


def _adaln_kernel(c_ref, w_ref, b_ref, o_ref):
    c = c_ref[...]
    a = (c * jax.nn.sigmoid(c)).astype(BF16)
    o_ref[...] = jnp.dot(a, w_ref[...].astype(BF16), preferred_element_type=F32) + b_ref[...]


def _adaln(cond, w, b, layer, *, tn=512):
    R, D = cond.shape
    N = w.shape[2]
    tn = _divisor(N, tn, LANES)
    vmem = 2 * _nbytes((D, tn), F32) + _nbytes((D, tn), BF16) + (4 << 20)
    out = pl.pallas_call(
        _adaln_kernel,
        out_shape=jax.ShapeDtypeStruct((R, N), F32),
        grid=(N // tn,),
        in_specs=[pl.BlockSpec((R, D), lambda j: (0, 0)),
                  pl.BlockSpec((None, D, tn), lambda j: (layer, 0, j)),
                  pl.BlockSpec((None, 1, tn), lambda j: (layer, 0, j))],
        out_specs=pl.BlockSpec((R, tn), lambda j: (0, j)),
        compiler_params=_params(("arbitrary",), vmem),
        name="adaln",
    )(cond, w, b.reshape(b.shape[0], 1, N))
    return out.reshape(R, 6, D)


def _stream_specs(n_ctx_tiles, tr, D):
    ctx = pl.BlockSpec((1, tr, D), lambda b, r: (b, jnp.minimum(r, n_ctx_tiles - 1), 0))
    lat = pl.BlockSpec((1, tr, D), lambda b, r: (b, jnp.maximum(r - n_ctx_tiles, 0), 0))
    return ctx, lat


def _modulate_kernel(c_ref, x_ref, m_ref, o_ref, *, n_ctx_tiles, shift, scale):
    m = m_ref[0]

    def emit(h_ref):
        o_ref[0] = (h_ref[0] * (1.0 + m[scale:scale + 1]) + m[shift:shift + 1]).astype(o_ref.dtype)

    pl.when(pl.program_id(1) < n_ctx_tiles)(lambda: emit(c_ref))
    pl.when(pl.program_id(1) >= n_ctx_tiles)(lambda: emit(x_ref))


def _modulate(ctx, x, mods, *, shift, scale):
    B, C, D = ctx.shape
    S = C + x.shape[1]
    tr = _divisor(math.gcd(S, C), ROW_TILE, 16)
    nct = C // tr
    return pl.pallas_call(
        functools.partial(_modulate_kernel, n_ctx_tiles=nct, shift=shift, scale=scale),
        out_shape=jax.ShapeDtypeStruct((B, S, D), BF16),
        grid=(B, S // tr),
        in_specs=[*_stream_specs(nct, tr, D),
                  pl.BlockSpec((1, 6, D), lambda b, r: (jnp.where(r < nct, B, b), 0, 0))],
        out_specs=pl.BlockSpec((1, tr, D), lambda b, r: (b, r, 0)),
        compiler_params=_params(("parallel", "arbitrary"), 0),
        name="modulate",
    )(ctx, x, mods)


def _layer_norm_rows(z, g, b):
    mu = jnp.mean(z, axis=-1, keepdims=True)
    zc = z - mu
    var = jnp.mean(zc * zc, axis=-1, keepdims=True)
    return zc * lax.rsqrt(var + LN_EPS) * g + b


def _resid_ln_kernel(*refs, n_ctx_tiles, split_h, gate, shift, scale):
    h_refs, (y_ref, m_ref, mn_ref, g_ref, b_ref, hn_ref, f_ref) = refs[:-7], refs[-7:]
    m = m_ref[0]
    mn = mn_ref[0]

    def emit(h_ref):
        hn = _layer_norm_rows(ALPHA * h_ref[0] + m[gate:gate + 1] * y_ref[0], g_ref[...], b_ref[...])
        hn_ref[0] = hn
        f_ref[0] = (hn * (1.0 + mn[scale:scale + 1]) + mn[shift:shift + 1]).astype(f_ref.dtype)

    if split_h:
        pl.when(pl.program_id(1) < n_ctx_tiles)(lambda: emit(h_refs[0]))
        pl.when(pl.program_id(1) >= n_ctx_tiles)(lambda: emit(h_refs[1]))
    else:
        emit(h_refs[0])


def _resid_ln(h, y, mods, mods_next, ln_g, ln_b, *, n_ctx_rows, h_row_offset, gate, shift, scale, f_dtype=BF16):
    B, R, D = y.shape
    tr = _divisor(math.gcd(R, n_ctx_rows, h_row_offset), ROW_TILE, 16)
    nct = n_ctx_rows // tr
    off = h_row_offset // tr
    cond = lambda b, r: (jnp.where(r < nct, B, b), 0, 0)
    row = pl.BlockSpec((1, tr, D), lambda b, r: (b, r, 0))
    vec = pl.BlockSpec((1, D), lambda b, r: (0, 0))
    split_h = isinstance(h, tuple)
    if split_h:
        assert off == 0 and nct > 0
        h_arrays, h_specs = list(h), list(_stream_specs(nct, tr, D))
    else:
        h_arrays, h_specs = [h], [pl.BlockSpec((1, tr, D), lambda b, r: (b, r + off, 0))]
    return pl.pallas_call(
        functools.partial(_resid_ln_kernel, n_ctx_tiles=nct, split_h=split_h, gate=gate, shift=shift, scale=scale),
        out_shape=(jax.ShapeDtypeStruct((B, R, D), F32), jax.ShapeDtypeStruct((B, R, D), f_dtype)),
        grid=(B, R // tr),
        in_specs=[*h_specs, row, pl.BlockSpec((1, 6, D), cond), pl.BlockSpec((1, 6, D), cond), vec, vec],
        out_specs=(row, row),
        compiler_params=_params(("parallel", "arbitrary"), 48 << 20),
        name="resid_ln",
    )(*h_arrays, y, mods, mods_next, ln_g.reshape(1, D), ln_b.reshape(1, D))


def _gates_kernel(w_ref, a_ref, b_ref, o_ref):
    g = lax.dot_general(w_ref[...].astype(BF16), a_ref[0], (((1,), (1,)), ((), ())),
                        preferred_element_type=F32)
    o_ref[0] = g + b_ref[...]


def _mlstm_gates(a, w_t, layer, gate_b, *, first_row):
    B, S, D = a.shape
    G = gate_b.shape[0]
    assert first_row % G == 0
    ts = _divisor(S, 1088, LANES)
    return pl.pallas_call(
        _gates_kernel,
        out_shape=jax.ShapeDtypeStruct((B, G, S), F32),
        grid=(B, S // ts),
        in_specs=[pl.BlockSpec((None, G, D), lambda b, s: (layer, first_row // G, 0)),
                  pl.BlockSpec((1, ts, D), lambda b, s: (b, s, 0)),
                  pl.BlockSpec((G, 1), lambda b, s: (0, 0))],
        out_specs=pl.BlockSpec((1, G, ts), lambda b, s: (b, 0, s)),
        compiler_params=_params(("parallel", "arbitrary"), 0),
        name="mlstm_gates",
    )(w_t, a, gate_b.reshape(G, 1))


def _log_sigmoid(x):
    return jnp.minimum(x, 0.0) - jnp.log1p(jnp.exp(-jnp.abs(x)))


def _mlstm_scan_kernel(g_ref, q_ref, k_ref, v_ref, h_ref, c_ref, n_ref, m_ref, *, heads, dk, dv, chunk):
    direction = pl.program_id(1)
    step = pl.program_id(2)
    L = chunk

    @pl.when(step == 0)
    def _():
        c_ref[...] = jnp.zeros_like(c_ref)
        n_ref[...] = jnp.zeros_like(n_ref)
        m_ref[...] = jnp.zeros_like(m_ref)

    row = lax.broadcasted_iota(jnp.int32, (L, L), 0)
    col = lax.broadcasted_iota(jnp.int32, (L, L), 1)
    seen = jnp.where(direction == 0, col - row, row - col) <= 0
    eye = row == col

    def to_col(r):
        return jnp.sum(jnp.where(eye, r, 0.0), axis=1, keepdims=True)

    def to_row(c):
        return jnp.sum(jnp.where(eye, c, 0.0), axis=0, keepdims=True)

    for h in range(heads):
        ig = g_ref[0, pl.ds(direction * 2 * heads + h, 1), :]
        lf = _log_sigmoid(g_ref[0, pl.ds(direction * 2 * heads + heads + h, 1), :])
        q = q_ref[0, :, h * dk:(h + 1) * dk] * (dk ** -0.5)
        k = k_ref[0, :, h * dk:(h + 1) * dk]
        v = v_ref[0, :, h * dv:(h + 1) * dv]
        m_prev = m_ref[h]

        b_col = jnp.sum(jnp.where(seen, lf, 0.0), axis=1, keepdims=True)
        b_row = to_row(b_col)
        d = jnp.where(seen, b_col - b_row + ig, MASKED)
        inter = b_col + m_prev
        m_t = jnp.maximum(inter, jnp.max(d, axis=1, keepdims=True))
        s = lax.dot_general(q, k, (((1,), (1,)), ((), ())), preferred_element_type=F32) * jnp.exp(d - m_t)
        g_in = jnp.exp(inter - m_t)
        num = (g_in * jnp.dot(q, c_ref[h].astype(BF16), preferred_element_type=F32)
               + jnp.dot(s.astype(BF16), v, preferred_element_type=F32))
        qf = q.astype(F32)
        den = g_in * jnp.sum(qf * n_ref[h], axis=1, keepdims=True) + jnp.sum(s, axis=1, keepdims=True)
        h_ref[0, 0, :, h * dv:(h + 1) * dv] = (num / jnp.maximum(jnp.abs(den), jnp.exp(-m_t))).astype(h_ref.dtype)

        b_last = jnp.sum(lf, axis=1, keepdims=True)
        w_row = b_last - b_row + ig
        m_new = jnp.maximum(b_last + m_prev, jnp.max(w_row, axis=1, keepdims=True))
        decay = jnp.exp(b_last + m_prev - m_new)
        kw = k.astype(F32) * to_col(jnp.exp(w_row - m_new))
        c_ref[h] = decay * c_ref[h] + jnp.dot(kw.T.astype(BF16), v, preferred_element_type=F32)
        n_ref[h] = decay * n_ref[h] + jnp.sum(kw, axis=0, keepdims=True)
        m_ref[h] = m_new


def _mlstm_scan(p, gates, *, n_ctx_rows, heads, qk, vw):
    B, S, _ = p.shape
    L = _divisor(math.gcd(S, n_ctx_rows), ML_CHUNK, LANES)
    nc, ncc = S // L, n_ctx_rows // L
    dk, dv = qk // heads, vw // heads

    def chunk(d, j):
        back = jnp.where(j < ncc, ncc - 1 - j, nc - 1 - (j - ncc))
        return jnp.where(d == 0, j, back)

    vmem = (2 * (2 * _nbytes((L, qk), BF16) + _nbytes((L, vw), BF16) + _nbytes((L, vw), F32))
            + _nbytes((heads, dk, dv), F32) + (16 << 20))
    return pl.pallas_call(
        functools.partial(_mlstm_scan_kernel, heads=heads, dk=dk, dv=dv, chunk=L),
        out_shape=jax.ShapeDtypeStruct((2, B, S, vw), BF16),
        grid=(B, 2, nc),
        in_specs=[pl.BlockSpec((1, 4 * heads, L), lambda b, d, j: (b, 0, chunk(d, j))),
                  pl.BlockSpec((1, L, qk), lambda b, d, j: (b, chunk(d, j), 0)),
                  pl.BlockSpec((1, L, qk), lambda b, d, j: (b, chunk(d, j), 1)),
                  pl.BlockSpec((1, L, vw), lambda b, d, j: (b, chunk(d, j), (2 * qk) // vw))],
        out_specs=pl.BlockSpec((1, 1, L, vw), lambda b, d, j: (d, b, chunk(d, j), 0)),
        scratch_shapes=[pltpu.VMEM((heads, dk, dv), F32), pltpu.VMEM((heads, 1, dk), F32),
                        pltpu.VMEM((heads, 1, 1), F32)],
        compiler_params=_params(("parallel", "parallel", "arbitrary"), vmem),
        name="mlstm_scan",
    )(gates, p, p, p)


def _mlstm_norm_kernel(h_ref, o_ref, g_ref, out_ref, *, heads, dv):
    hs = h_ref[0, 0].astype(F32) + h_ref[1, 0].astype(F32)
    for h in range(heads):
        x = hs[:, h * dv:(h + 1) * dv]
        x = x * lax.rsqrt(jnp.mean(x * x, axis=-1, keepdims=True) + RMS_EPS)
        x = x * g_ref[:, h * dv:(h + 1) * dv] * jax.nn.sigmoid(o_ref[0, :, h * dv:(h + 1) * dv].astype(F32))
        out_ref[0, :, h * dv:(h + 1) * dv] = x.astype(out_ref.dtype)


def _mlstm_norm(hfb, p, norm_g, *, heads, o_col_block):
    _, B, S, vw = hfb.shape
    tr = _divisor(S, ROW_TILE, 16)
    return pl.pallas_call(
        functools.partial(_mlstm_norm_kernel, heads=heads, dv=vw // heads),
        out_shape=jax.ShapeDtypeStruct((B, S, vw), BF16),
        grid=(B, S // tr),
        in_specs=[pl.BlockSpec((2, 1, tr, vw), lambda b, r: (0, b, r, 0)),
                  pl.BlockSpec((1, tr, vw), lambda b, r: (b, r, o_col_block)),
                  pl.BlockSpec((1, vw), lambda b, r: (0, 0))],
        out_specs=pl.BlockSpec((1, tr, vw), lambda b, r: (b, r, 0)),
        compiler_params=_params(("parallel", "arbitrary"), 48 << 20),
        name="mlstm_norm",
    )(hfb, p, norm_g.reshape(1, vw))


def _qkv_rope_kernel(a_ref, w_ref, cos_ref, sin_ref, o_ref, *, rope_blocks):
    j = pl.program_id(1)
    acc = jnp.dot(a_ref[...], w_ref[...].astype(BF16), preferred_element_type=F32)

    @pl.when(j < rope_blocks)
    def _():
        tn = acc.shape[1]
        lane = lax.broadcasted_iota(jnp.int32, (1, LANES), 1)
        first = (lane // (LANES // 4)) % 2 == 0
        for hd in range(tn // LANES):
            x = acc[:, hd * LANES:(hd + 1) * LANES]
            partner = jnp.where(first, pltpu.roll(x, LANES - LANES // 4, 1), pltpu.roll(x, LANES // 4, 1))
            o_ref[:, hd * LANES:(hd + 1) * LANES] = (x * cos_ref[...] + partner * sin_ref[...]).astype(o_ref.dtype)

    @pl.when(j >= rope_blocks)
    def _():
        o_ref[...] = acc.astype(o_ref.dtype)


def _qkv_rope(a, w, layer, cos, sin, *, rope_cols, rows_per_batch, tn=512):
    M, K = a.shape
    N = w.shape[2]
    tm = _divisor(rows_per_batch, MM_ROWS, 16)
    tn = _divisor(math.gcd(N, rope_cols), tn, LANES)
    ntab = rows_per_batch // tm
    vmem = (2 * _nbytes((tm, K), BF16) + 2 * _nbytes((K, tn), F32) + _nbytes((K, tn), BF16)
            + 2 * _nbytes((tm, tn), BF16) + 3 * _nbytes((tm, tn), F32) + (6 << 20))
    return pl.pallas_call(
        functools.partial(_qkv_rope_kernel, rope_blocks=rope_cols // tn),
        out_shape=jax.ShapeDtypeStruct((M, N), BF16),
        grid=(M // tm, N // tn),
        in_specs=[pl.BlockSpec((tm, K), lambda i, j: (i, 0)),
                  pl.BlockSpec((None, K, tn), lambda i, j: (layer, 0, j)),
                  pl.BlockSpec((tm, LANES), lambda i, j: (i % ntab, 0)),
                  pl.BlockSpec((tm, LANES), lambda i, j: (i % ntab, 0))],
        out_specs=pl.BlockSpec((tm, tn), lambda i, j: (i, j)),
        compiler_params=_params(("parallel", "arbitrary"), vmem),
        name="qkv_rope",
    )(a, w, cos, sin)


def _attention_kernel(sink_ref, q_ref, kp_ref, kc_ref, kn_ref, kx_ref, vp_ref, vc_ref, vn_ref, vx_ref, o_ref,
                      *, group, kv_step, n_blocks, scale):
    n = pl.program_id(2)
    T = q_ref.shape[1]
    row = lax.broadcasted_iota(jnp.int32, (group * T, T), 0) % T
    col = lax.broadcasted_iota(jnp.int32, (group * T, T), 1)
    prev_ok = (col >= row) & (n > 0)
    next_ok = (col <= row) & (n < n_blocks - 1)
    for h in range(kv_step):
        kv = pl.program_id(1) * kv_step + h
        hs = slice(h * LANES, (h + 1) * LANES)
        q = jnp.concatenate([q_ref[0, :, (h * group + g) * LANES:(h * group + g + 1) * LANES] for g in range(group)],
                            axis=0)
        k = jnp.concatenate([kp_ref[0, :, hs], kc_ref[0, :, hs], kn_ref[0, :, hs], kx_ref[0, :, hs]], axis=0)
        v = jnp.concatenate([vp_ref[0, :, hs], vc_ref[0, :, hs], vn_ref[0, :, hs], vx_ref[0, :, hs]], axis=0)
        s = lax.dot_general(q, k, (((1,), (1,)), ((), ())), preferred_element_type=F32) * scale
        s_p = jnp.where(prev_ok, s[:, :T], MASKED)
        s_c = s[:, T:2 * T]
        s_n = jnp.where(next_ok, s[:, 2 * T:3 * T], MASKED)
        s_x = s[:, 3 * T:]
        sink = jnp.concatenate([jnp.full((T, 1), sink_ref[kv * group + g], F32) for g in range(group)], axis=0)
        m = jnp.maximum(jnp.maximum(jnp.max(s_p, axis=1, keepdims=True), jnp.max(s_c, axis=1, keepdims=True)),
                        jnp.maximum(jnp.max(s_n, axis=1, keepdims=True), jnp.max(s_x, axis=1, keepdims=True)))
        m = jnp.maximum(m, sink)
        e_p, e_c, e_n, e_x = jnp.exp(s_p - m), jnp.exp(s_c - m), jnp.exp(s_n - m), jnp.exp(s_x - m)
        total = (jnp.sum(e_p, axis=1, keepdims=True) + jnp.sum(e_c, axis=1, keepdims=True)
                 + jnp.sum(e_n, axis=1, keepdims=True) + jnp.sum(e_x, axis=1, keepdims=True) + jnp.exp(sink - m))
        inv = 1.0 / total
        p = jnp.concatenate([(e_p * inv).astype(BF16), (e_c * inv).astype(BF16), (e_n * inv).astype(BF16),
                             (e_x * inv).astype(BF16)], axis=1)
        o = jnp.dot(p, v, preferred_element_type=F32)
        for g in range(group):
            o_ref[0, :, (h * group + g) * LANES:(h * group + g + 1) * LANES] = o[g * T:(g + 1) * T].astype(o_ref.dtype)


def _window_attention(p, sink, *, n_ctx_rows, heads, kv_heads):
    B, S, _ = p.shape
    T = S - n_ctx_rows
    assert WINDOW == Q_BLOCK and T % Q_BLOCK == 0 and n_ctx_rows % Q_BLOCK == 0
    nb = T // Q_BLOCK
    cb = n_ctx_rows // Q_BLOCK
    group = heads // kv_heads
    kv_step = 4 if kv_heads % 4 == 0 else 1
    kw = kv_step * LANES
    kcol, vcol = heads // kv_step, (heads + kv_heads) // kv_step

    def blk(colbase, shift):
        def index(b, kv, n):
            return (b, jnp.clip(n + shift, 0, nb - 1) + cb, colbase + kv)
        return pl.BlockSpec((1, Q_BLOCK, kw), index)

    def ctx(colbase):
        return pl.BlockSpec((1, n_ctx_rows, kw), lambda b, kv, n: (b, 0, colbase + kv))

    qw = kv_step * group * LANES
    return pl.pallas_call(
        functools.partial(_attention_kernel, group=group, kv_step=kv_step, n_blocks=nb, scale=LANES ** -0.5),
        out_shape=jax.ShapeDtypeStruct((B, T, heads * LANES), BF16),
        grid=(B, kv_heads // kv_step, nb),
        in_specs=[pl.BlockSpec(memory_space=pltpu.SMEM),
                  pl.BlockSpec((1, Q_BLOCK, qw), lambda b, kv, n: (b, n + cb, kv)),
                  blk(kcol, -1), blk(kcol, 0), blk(kcol, 1), ctx(kcol),
                  blk(vcol, -1), blk(vcol, 0), blk(vcol, 1), ctx(vcol)],
        out_specs=pl.BlockSpec((1, Q_BLOCK, qw), lambda b, kv, n: (b, n, kv)),
        compiler_params=_params(("parallel", "parallel", "arbitrary"), 0),
        name="window_attention",
    )(sink, p, p, p, p, p, p, p, p, p)


def _rope_tables(n_ctx_rows, seq):
    quarter = LANES // 4
    freqs = ROPE_THETA ** (-jnp.arange(quarter, dtype=F32) / quarter)
    t = jnp.arange(seq)
    ang_r = (t // GRID_W).astype(F32)[:, None] * freqs[None, :]
    ang_c = (t % GRID_W).astype(F32)[:, None] * freqs[None, :]
    cos = jnp.concatenate([jnp.cos(ang_r), jnp.cos(ang_r), jnp.cos(ang_c), jnp.cos(ang_c)], axis=-1)
    sin = jnp.concatenate([-jnp.sin(ang_r), jnp.sin(ang_r), -jnp.sin(ang_c), jnp.sin(ang_c)], axis=-1)
    cos = jnp.concatenate([jnp.ones((n_ctx_rows, LANES), F32), cos], axis=0)
    sin = jnp.concatenate([jnp.zeros((n_ctx_rows, LANES), F32), sin], axis=0)
    return cos, sin


def _router_kernel(w_ref, f_ref, idx_ref, wt_ref):
    logits = lax.dot_general(w_ref[...].astype(BF16), f_ref[...].astype(BF16), (((1,), (1,)), ((), ())),
                             preferred_element_type=F32)
    n_exp = logits.shape[0]
    eid = lax.broadcasted_iota(jnp.int32, logits.shape, 0)
    v1 = jnp.max(logits, axis=0, keepdims=True)
    i1 = jnp.min(jnp.where(logits == v1, eid, n_exp), axis=0, keepdims=True)
    rest = jnp.where(eid == i1, -jnp.inf, logits)
    v2 = jnp.max(rest, axis=0, keepdims=True)
    i2 = jnp.min(jnp.where(rest == v2, eid, n_exp), axis=0, keepdims=True)
    e2 = jnp.exp(v2 - v1)
    idx_ref[0:1, :] = i1
    idx_ref[1:2, :] = i2
    wt_ref[0:1, :] = 1.0 / (1.0 + e2)
    wt_ref[1:2, :] = e2 / (1.0 + e2)


def _router(f, w_router_t):
    T, D = f.shape
    E = w_router_t.shape[0]
    tt = _divisor(T, 1024, LANES)
    return pl.pallas_call(
        _router_kernel,
        out_shape=(jax.ShapeDtypeStruct((TOP_K, T), jnp.int32), jax.ShapeDtypeStruct((TOP_K, T), F32)),
        grid=(T // tt,),
        in_specs=[pl.BlockSpec((E, D), lambda i: (0, 0)), pl.BlockSpec((tt, D), lambda i: (i, 0))],
        out_specs=(pl.BlockSpec((TOP_K, tt), lambda i: (0, i)), pl.BlockSpec((TOP_K, tt), lambda i: (0, i))),
        compiler_params=_params(("arbitrary",), 48 << 20),
        name="moe_router",
    )(w_router_t, f)


GATHER_ROWS = MOE_ROWS // MOE_PARTS
GATHER_UNROLL = 8


def _gather_kernel(idx_ref, tp_ref, src_ref, o_ref, stage, sems):
    i = pl.program_id(0)
    last = pl.num_programs(0) - 1

    def used(step):
        return (step % MOE_PARTS) < tp_ref[step // MOE_PARTS]

    def issue(step, slot):
        def body(r, carry):
            row = idx_ref[step * GATHER_ROWS + r]
            pltpu.make_async_copy(src_ref.at[pl.ds(row, 1), :], stage.at[slot, pl.ds(r, 1), :], sems.at[slot]).start()
            return carry

        lax.fori_loop(0, GATHER_ROWS, body, 0, unroll=GATHER_UNROLL)

    @pl.when(jnp.logical_and(i == 0, used(0)))
    def _():
        issue(0, 0)

    nxt = jnp.minimum(i + 1, last)

    @pl.when(jnp.logical_and(i < last, used(nxt)))
    def _():
        issue(nxt, nxt % 2)

    @pl.when(used(i))
    def _():
        slot = i % 2
        pltpu.make_async_copy(src_ref.at[pl.ds(0, GATHER_ROWS), :], stage.at[slot], sems.at[slot]).wait()
        o_ref[...] = stage[slot].astype(o_ref.dtype)

    @pl.when(jnp.logical_not(used(i)))
    def _():
        o_ref[...] = jnp.zeros_like(o_ref)


def _gather_rows(src, idx, tile_parts, out_dtype):
    N, D = src.shape
    R = idx.shape[0]
    assert R % MOE_ROWS == 0 and D % LANES == 0
    return pl.pallas_call(
        _gather_kernel,
        out_shape=jax.ShapeDtypeStruct((R, D), out_dtype),
        grid_spec=pltpu.PrefetchScalarGridSpec(
            num_scalar_prefetch=2,
            grid=(R // GATHER_ROWS,),
            in_specs=[pl.BlockSpec(memory_space=pl.ANY)],
            out_specs=pl.BlockSpec((GATHER_ROWS, D), lambda i, idx_ref, tp_ref: (i, 0)),
            scratch_shapes=[pltpu.VMEM((2, GATHER_ROWS, D), src.dtype), pltpu.SemaphoreType.DMA((2,))],
        ),
        compiler_params=_params(("arbitrary",), 0),
        name="gather_rows",
    )(idx, tile_parts, src)


def _for_used_parts(parts, o_ref, emit):
    part = o_ref.shape[0] // MOE_PARTS
    for n in range(MOE_PARTS + 1):
        @pl.when(parts == n)
        def _(n=n):
            if n > 0:
                emit(slice(0, n * part))
            if n < MOE_PARTS:
                o_ref[n * part:, :] = jnp.zeros(((MOE_PARTS - n) * part, o_ref.shape[1]), o_ref.dtype)


def _expert_up_kernel(te_ref, tp_ref, ts_ref, x_ref, wg_ref, wu_ref, o_ref, wgb_ref, wub_ref):
    i = pl.program_id(1)
    fresh = jnp.logical_or(i == 0, te_ref[i] != te_ref[jnp.maximum(i - 1, 0)])

    @pl.when(fresh)
    def _():
        wgb_ref[...] = wg_ref[...].astype(BF16)
        wub_ref[...] = wu_ref[...].astype(BF16)

    def emit(rows):
        x = x_ref[rows, :]
        g = jnp.dot(x, wgb_ref[...], preferred_element_type=F32)
        u = jnp.dot(x, wub_ref[...], preferred_element_type=F32)
        o_ref[rows, :] = (g * jax.nn.sigmoid(g) * u).astype(o_ref.dtype)

    _for_used_parts(tp_ref[i], o_ref, emit)


def _expert_up(xs, w_gu, layer, tiles, *, tn=256):
    P, D = xs.shape
    F2 = w_gu.shape[3]
    F = F2 // 2
    tm = MOE_ROWS
    tn = _divisor(F, tn, LANES)
    nf = F // tn
    vmem = (2 * _nbytes((tm, D), BF16) + 4 * _nbytes((D, tn), F32) + 2 * _nbytes((D, tn), BF16)
            + 2 * _nbytes((tm, tn), BF16) + 3 * _nbytes((tm, tn), F32) + (4 << 20))
    return pl.pallas_call(
        _expert_up_kernel,
        out_shape=jax.ShapeDtypeStruct((P, F), BF16),
        grid_spec=pltpu.PrefetchScalarGridSpec(
            num_scalar_prefetch=3,
            grid=(nf, P // tm),
            in_specs=[pl.BlockSpec((tm, D), lambda j, i, te, th, ts: (ts[i], 0)),
                      pl.BlockSpec((None, None, D, tn), lambda j, i, te, th, ts: (layer, te[i], 0, j)),
                      pl.BlockSpec((None, None, D, tn), lambda j, i, te, th, ts: (layer, te[i], 0, j + nf))],
            out_specs=pl.BlockSpec((tm, tn), lambda j, i, te, th, ts: (i, j)),
            scratch_shapes=[pltpu.VMEM((D, tn), BF16), pltpu.VMEM((D, tn), BF16)],
        ),
        compiler_params=_params(("arbitrary", "arbitrary"), vmem),
        name="expert_up",
    )(*tiles, xs, w_gu, w_gu)


def _expert_down_kernel(te_ref, tp_ref, ts_ref, a_ref, w_ref, o_ref, wb_ref):
    i = pl.program_id(1)
    fresh = jnp.logical_or(i == 0, te_ref[i] != te_ref[jnp.maximum(i - 1, 0)])

    @pl.when(fresh)
    def _():
        wb_ref[...] = w_ref[...].astype(BF16)

    def emit(rows):
        o_ref[rows, :] = jnp.dot(a_ref[rows, :], wb_ref[...], preferred_element_type=F32)

    _for_used_parts(tp_ref[i], o_ref, emit)


def _expert_down(act, w_down, layer, tiles, *, tn=512):
    P, F = act.shape
    D = w_down.shape[3]
    tm = MOE_ROWS
    tn = _divisor(D, tn, LANES)
    vmem = (2 * _nbytes((tm, F), BF16) + 2 * _nbytes((F, tn), F32) + _nbytes((F, tn), BF16)
            + 3 * _nbytes((tm, tn), F32) + (4 << 20))
    return pl.pallas_call(
        _expert_down_kernel,
        out_shape=jax.ShapeDtypeStruct((P, D), F32),
        grid_spec=pltpu.PrefetchScalarGridSpec(
            num_scalar_prefetch=3,
            grid=(D // tn, P // tm),
            in_specs=[pl.BlockSpec((tm, F), lambda j, i, te, th, ts: (ts[i], 0)),
                      pl.BlockSpec((None, None, F, tn), lambda j, i, te, th, ts: (layer, te[i], 0, j))],
            out_specs=pl.BlockSpec((tm, tn), lambda j, i, te, th, ts: (i, j)),
            scratch_shapes=[pltpu.VMEM((F, tn), BF16)],
        ),
        compiler_params=_params(("arbitrary", "arbitrary"), vmem),
        name="expert_down",
    )(*tiles, act, w_down)


def _routing_tables(idx, n_experts):
    K, T = idx.shape
    tm = MOE_ROWS
    n_tiles = (K * T) // tm + n_experts
    P = n_tiles * tm
    e = idx.reshape(-1)
    onehot = (e[:, None] == jnp.arange(n_experts)[None, :]).astype(jnp.int32)
    csum = jnp.cumsum(onehot, axis=0)
    rank = jnp.sum(csum * onehot, axis=1) - 1
    counts = csum[-1]
    tiles_per = (counts + tm - 1) // tm
    tile_end = jnp.cumsum(tiles_per)
    start = (tile_end - tiles_per) * tm
    slot = jnp.sum(onehot * start[None, :], axis=1) + rank
    token = jnp.tile(jnp.arange(T, dtype=jnp.int32), K)
    slot_token = jnp.zeros((P,), jnp.int32).at[slot].set(token)
    tiles = jnp.arange(n_tiles, dtype=jnp.int32)
    owner = jnp.sum((tile_end[None, :] <= tiles[:, None]).astype(jnp.int32), axis=1)
    owns = (owner[:, None] == jnp.arange(n_experts)[None, :]).astype(jnp.int32)
    rows_used = jnp.clip(jnp.sum(owns * (start + counts)[None, :], axis=1) - tiles * tm, 0, tm)
    part = tm // MOE_PARTS
    tile_parts = ((rows_used + part - 1) // part).astype(jnp.int32)
    last_owner = jnp.sum((tile_end <= tile_end[-1] - 1).astype(jnp.int32))
    tile_expert = jnp.where(rows_used > 0, owner, last_owner).astype(jnp.int32)
    tile_src = jnp.minimum(tiles, tile_end[-1] - 1).astype(jnp.int32)
    return slot_token, slot.reshape(K, T).astype(jnp.int32), (tile_expert, tile_parts, tile_src)


def _combine_ln_kernel(slot_ref, ys_ref, h_ref, w_ref, m_ref, g_ref, b_ref, o_ref, ybuf, sems, *, gate, n_tokens):
    s = pl.program_id(0)
    tr = o_ref.shape[0]

    def issue(step, buf):
        def body(r, carry):
            for k in range(TOP_K):
                row = slot_ref[k * n_tokens + step * tr + r]
                pltpu.make_async_copy(ys_ref.at[pl.ds(row, 1), :], ybuf.at[buf, k, pl.ds(r, 1), :], sems.at[buf]).start()
            return carry
        lax.fori_loop(0, tr, body, 0, unroll=GATHER_UNROLL // TOP_K)

    @pl.when(s == 0)
    def _():
        issue(0, 0)

    @pl.when(s + 1 < pl.num_programs(0))
    def _():
        issue(s + 1, (s + 1) % 2)

    buf = s % 2
    for k in range(TOP_K):
        pltpu.make_async_copy(ys_ref.at[pl.ds(0, tr), :], ybuf.at[buf, k], sems.at[buf]).wait()
    y = w_ref[:, 0:1] * ybuf[buf, 0]
    for k in range(1, TOP_K):
        y = y + w_ref[:, k:k + 1] * ybuf[buf, k]
    m = m_ref[0]
    o_ref[...] = _layer_norm_rows(ALPHA * h_ref[...] + m[gate:gate + 1] * y, g_ref[...], b_ref[...])


def _combine_ln(h, ys, pair_slot, wts, mods, ln_g, ln_b, *, gate):
    B, T, D = h.shape
    n_tokens = B * T
    tr = _divisor(T, ROW_TILE, 16)
    row = pl.BlockSpec((tr, D), lambda i, slots: (i, 0))
    vec = pl.BlockSpec((1, D), lambda i, slots: (0, 0))
    out = pl.pallas_call(
        functools.partial(_combine_ln_kernel, gate=gate, n_tokens=n_tokens),
        out_shape=jax.ShapeDtypeStruct((n_tokens, D), F32),
        grid_spec=pltpu.PrefetchScalarGridSpec(
            num_scalar_prefetch=1,
            grid=(n_tokens // tr,),
            in_specs=[pl.BlockSpec(memory_space=pl.ANY), row,
                      pl.BlockSpec((tr, TOP_K), lambda i, slots: (i, 0)),
                      pl.BlockSpec((1, 6, D), lambda i, slots: (i // (T // tr), 0, 0)), vec, vec],
            out_specs=row,
            scratch_shapes=[pltpu.VMEM((2, TOP_K, tr, D), F32), pltpu.SemaphoreType.DMA((2,))],
        ),
        compiler_params=_params(("arbitrary",), 2 * TOP_K * _nbytes((tr, D), F32) + 5 * _nbytes((tr, D), F32) + (4 << 20)),
        name="combine_ln",
    )(pair_slot.reshape(-1), ys, h.reshape(n_tokens, D), wts.T, mods, ln_g.reshape(1, D), ln_b.reshape(1, D))
    return out.reshape(B, T, D)


SHIFT_MIX, SCALE_MIX, GATE_MIX, SHIFT_FFN, SCALE_FFN, GATE_FFN = range(6)


def kernel(x, c, ctx, c_ctx, ada_w, ada_b, ln_g, ln_b, ml_w_in, ml_gate_b, ml_norm_g, ml_w_out, att_w_in, att_sink,
           att_w_out, ffn_w_gu, ffn_w_down, moe_router, moe_w_gu, moe_w_down):
    B, T, D = x.shape
    C = ctx.shape[1]
    S = C + T
    assert ada_w.shape[0] == DEPTH == 2
    qk, vw = ml_w_out.shape[1] // 2, ml_w_out.shape[1]
    n_experts = moe_router.shape[-1]

    cond = jnp.concatenate([c, c_ctx[None, :]], axis=0)
    mods = [_adaln(cond, ada_w, ada_b, 0), None]

    a = _modulate(ctx, x, mods[0], shift=SHIFT_MIX, scale=SCALE_MIX)
    n_main = 2 * qk + 2 * vw
    w_in_t = jnp.swapaxes(ml_w_in, 1, 2)
    p = _matmul(a.reshape(B * S, D), w_in_t, 0, n_out=n_main, out_dtype=BF16, w_transposed=True,
                name="mlstm_in").reshape(B, S, n_main)
    gates = _mlstm_gates(a, w_in_t, 0, ml_gate_b[0], first_row=n_main)
    hfb = _mlstm_scan(p, gates, n_ctx_rows=C, heads=ML_HEADS, qk=qk, vw=vw)
    hn = _mlstm_norm(hfb, p, ml_norm_g[0], heads=ML_HEADS, o_col_block=(2 * qk + vw) // vw)
    y = _matmul(hn.reshape(B * S, vw), ml_w_out, 0, out_dtype=BF16, name="mlstm_out").reshape(B, S, D)
    h1, f = _resid_ln((ctx, x), y, mods[0], mods[0], ln_g[0, 0], ln_b[0, 0], n_ctx_rows=C, h_row_offset=0,
                      gate=GATE_MIX, shift=SHIFT_FFN, scale=SCALE_FFN)
    act, mods[1] = _swiglu_up(f.reshape(B * S, D), ffn_w_gu, 0, cond, ada_w, ada_b, 1)
    y = _matmul(act, _cast_bf16(ffn_w_down), 0, out_dtype=BF16, tn=256, a_buffers=1,
                name="ffn_down").reshape(B, S, D)
    h2, a = _resid_ln(h1, y, mods[0], mods[1], ln_g[0, 1], ln_b[0, 1], n_ctx_rows=C, h_row_offset=0,
                      gate=GATE_FFN, shift=SHIFT_MIX, scale=SCALE_MIX)

    q_cols = ATT_HEADS * LANES
    cos, sin = _rope_tables(C, T)
    p = _qkv_rope(a.reshape(B * S, D), att_w_in, 0, cos, sin, rope_cols=q_cols + ATT_KV * LANES,
                  rows_per_batch=S).reshape(B, S, -1)
    o = _window_attention(p, att_sink[0], n_ctx_rows=C, heads=ATT_HEADS, kv_heads=ATT_KV)
    y = _matmul(o.reshape(B * T, q_cols), att_w_out, 0, out_dtype=BF16, name="att_out").reshape(B, T, D)
    h3, f = _resid_ln(h2, y, mods[1], mods[1], ln_g[1, 0], ln_b[1, 0], n_ctx_rows=0, h_row_offset=C,
                      gate=GATE_MIX, shift=SHIFT_FFN, scale=SCALE_FFN, f_dtype=F32)

    f2 = f.reshape(B * T, D)
    idx, wts = _router(f2, moe_router[0].T)
    slot_token, pair_slot, tiles = _routing_tables(idx, n_experts)
    xs = _gather_rows(f2, slot_token, tiles[1], BF16)
    act = _expert_up(xs, moe_w_gu, 0, tiles)
    ys = _expert_down(act, moe_w_down, 0, tiles)
    return _combine_ln(h3, ys, pair_slot, wts, mods[1], ln_g[1, 1], ln_b[1, 1], gate=GATE_FFN)
```
